```python
import math
import jax, jax.numpy as jnp
from jax import lax
import numpy as np

D_MODEL = 2048
BATCH = 16
SEQ = 2048
DEPTH = 4

N_MIXERS = 2
N_HGRN_LAYERS = (DEPTH + 1) // 2
N_MAMBA_LAYERS = DEPTH // 2
NORM_EPS = 1e-5

HGRN_EXPAND = 128
HGRN_HEADS = D_MODEL // HGRN_EXPAND
HGRN_DK = HGRN_EXPAND
HGRN_DV = D_MODEL // HGRN_HEADS
HGRN_FDIM = HGRN_HEADS * HGRN_DK
HGRN_IN_DIM = 2 * HGRN_FDIM + 2 * HGRN_HEADS * HGRN_DV
HGRN_CHUNK = 64

M_EXPAND = 2
M_D_INNER = M_EXPAND * D_MODEL
M_HEADDIM = 64
M_HEADS = M_D_INNER // M_HEADDIM
M_GROUPS = 8
M_HPG = M_HEADS // M_GROUPS
M_D_STATE = 128
M_CONV = 4
M_CONV_DIM = M_D_INNER + 2 * M_GROUPS * M_D_STATE
M_IN_DIM = M_D_INNER + M_CONV_DIM + M_HEADS
M_CHUNK = 128

D_FF = 5632
FFN_CONV = 3

kernel_name = 'hybrid_hgrn2_mamba2_convffn'


def rms_norm(x, w):
    xf = x.astype(jnp.float32)
    y = xf * lax.rsqrt(jnp.mean(xf * xf, axis=-1, keepdims=True) + NORM_EPS)
    return (y * w.astype(jnp.float32)).astype(x.dtype)


def causal_dwconv(x, w, b):
    K, C = w.shape
    y = lax.conv_general_dilated(x, w[:, None, :].astype(x.dtype), window_strides=(1,),
                                 padding=[(K - 1, 0)], dimension_numbers=('NWC', 'WIO', 'NWC'),
                                 feature_group_count=C)
    return y + b.astype(x.dtype)


def masked_exp(mask, logits):
    return jnp.where(mask, jnp.exp(jnp.where(mask, logits, 0.0)), 0.0)


def hgrn2_mixer(u, w_in, lb, gn_w, w_out):
    Bsz, L, _ = u.shape
    n_chunks = L // HGRN_CHUNK
    f32 = jnp.float32
    q, f, v, g = jnp.split(u @ w_in, [HGRN_FDIM, 2 * HGRN_FDIM, 2 * HGRN_FDIM + HGRN_HEADS * HGRN_DV], axis=-1)
    q = jax.nn.silu(q.astype(f32))
    f = f.astype(f32)
    lb = lb.astype(f32)
    log_f = jnp.log(lb + (1.0 - lb) * jax.nn.sigmoid(f))
    k = (1.0 - lb) * jax.nn.sigmoid(-f)
    v = v.astype(f32)

    def to_chunks(t, d):
        return t.reshape(Bsz, n_chunks, HGRN_CHUNK, HGRN_HEADS, d).transpose(1, 0, 3, 2, 4)

    causal = jnp.tril(jnp.ones((HGRN_CHUNK, HGRN_CHUNK), bool))[..., None]

    def chunk_step(S, inp):
        qc, kc, vc, gc = inp
        b = jnp.cumsum(gc, axis=2)
        diff = b[:, :, :, None, :] - b[:, :, None, :, :]
        decay = masked_exp(causal, diff)
        A = jnp.einsum('bhik,bhjk,bhijk->bhij', qc, kc, decay)
        o = jnp.einsum('bhij,bhjv->bhiv', A, vc) + jnp.einsum('bhik,bhkv->bhiv', qc * jnp.exp(b), S)
        b_last = b[:, :, -1:, :]
        S = jnp.exp(b_last[:, :, 0, :, None]) * S + jnp.einsum('bhjk,bhjv->bhkv', kc * jnp.exp(b_last - b), vc)
        return S, o

    S0 = jnp.zeros((Bsz, HGRN_HEADS, HGRN_DK, HGRN_DV), f32)
    _, o = lax.scan(chunk_step, S0, (to_chunks(q, HGRN_DK), to_chunks(k, HGRN_DK),
                                     to_chunks(v, HGRN_DV), to_chunks(log_f, HGRN_DK)))
    o = o.transpose(1, 0, 3, 2, 4).reshape(Bsz, L, HGRN_HEADS, HGRN_DV)
    g = g.astype(f32).reshape(Bsz, L, HGRN_HEADS, HGRN_DV)
    o = o * lax.rsqrt(jnp.mean(o * o, axis=-1, keepdims=True) + NORM_EPS) * gn_w.astype(f32) * jax.nn.silu(g)
    return o.reshape(Bsz, L, HGRN_HEADS * HGRN_DV).astype(u.dtype) @ w_out


def mamba2_mixer(u, w_in, conv_w, conv_b, dt_bias, A_log, D_skip, norm_w, w_out):
    Bsz, L, _ = u.shape
    nc = L // M_CHUNK
    f32 = jnp.float32
    z, xBC, dt = jnp.split(u @ w_in, [M_D_INNER, M_D_INNER + M_CONV_DIM], axis=-1)
    xBC = jax.nn.silu(causal_dwconv(xBC, conv_w, conv_b)).astype(f32)
    xs, Bm, Cm = jnp.split(xBC, [M_D_INNER, M_D_INNER + M_GROUPS * M_D_STATE], axis=-1)
    dt = jax.nn.softplus(dt.astype(f32) + dt_bias.astype(f32))
    A = -jnp.exp(A_log.astype(f32))
    xh = xs.reshape(Bsz, L, M_HEADS, M_HEADDIM)
    X = (xh * dt[..., None]).reshape(Bsz, nc, M_CHUNK, M_GROUPS, M_HPG, M_HEADDIM)
    Ad = (dt * A).reshape(Bsz, nc, M_CHUNK, M_GROUPS, M_HPG).transpose(0, 3, 4, 1, 2)
    Bc = Bm.reshape(Bsz, nc, M_CHUNK, M_GROUPS, M_D_STATE)
    Cc = Cm.reshape(Bsz, nc, M_CHUNK, M_GROUPS, M_D_STATE)
    a_cs = jnp.cumsum(Ad, axis=-1)
    causal = jnp.tril(jnp.ones((M_CHUNK, M_CHUNK), bool))
    Lmat = masked_exp(causal, a_cs[..., :, None] - a_cs[..., None, :])
    CB = jnp.einsum('bclgn,bcsgn->bcgls', Cc, Bc)
    y_diag = jnp.einsum('bcgls,bgjcls,bcsgjp->bclgjp', CB, Lmat, X)
    decay_states = jnp.exp(a_cs[..., -1:] - a_cs)
    states = jnp.einsum('bcsgn,bgjcs,bcsgjp->cbgjpn', Bc, decay_states, X)
    chunk_decay = jnp.exp(a_cs[..., -1]).transpose(3, 0, 1, 2)

    def state_step(h, inp):
        st, dec = inp
        return dec[..., None, None] * h + st, h

    h0 = jnp.zeros((Bsz, M_GROUPS, M_HPG, M_HEADDIM, M_D_STATE), f32)
    _, h_in = lax.scan(state_step, h0, (states, chunk_decay))
    y_off = jnp.einsum('bclgn,cbgjpn,bgjcl->bclgjp', Cc, h_in, jnp.exp(a_cs))
    y = (y_diag + y_off).reshape(Bsz, L, M_HEADS, M_HEADDIM) + xh * D_skip.astype(f32)[:, None]
    y = y.reshape(Bsz, L, M_D_INNER) * jax.nn.silu(z.astype(f32))
    y = y.reshape(Bsz, L, M_GROUPS, M_D_INNER // M_GROUPS)
    y = y * lax.rsqrt(jnp.mean(y * y, axis=-1, keepdims=True) + NORM_EPS)
    y = y.reshape(Bsz, L, M_D_INNER) * norm_w.astype(f32)
    return y.astype(u.dtype) @ w_out


def conv_ffn(u, w_up, conv_w, conv_b, w_down):
    h = causal_dwconv(u @ w_up, conv_w, conv_b)
    g, up = jnp.split(h, 2, axis=-1)
    return (jax.nn.silu(g) * up) @ w_down


def setup_inputs(seed: int = 0) -> dict:
    key = jax.random.key(seed)
    ks = jax.random.split(key, 24)
    f32 = jnp.float32
    nh, nm = N_HGRN_LAYERS, N_MAMBA_LAYERS

    def dense(k, shape):
        return jax.random.normal(k, shape, f32) * shape[-2] ** -0.5

    def gain(k, shape):
        return 1.0 + 0.02 * jax.random.normal(k, shape, f32)

    def small(k, shape):
        return 0.02 * jax.random.normal(k, shape, f32)

    dt0 = jnp.exp(jax.random.uniform(ks[10], (nm, M_HEADS), f32, math.log(1e-3), math.log(1e-1)))
    return {
        'x': jax.random.normal(ks[0], (BATCH, SEQ, D_MODEL), f32),
        'mix_norm': gain(ks[1], (DEPTH, D_MODEL)),
        'ffn_norm': gain(ks[2], (DEPTH, D_MODEL)),
        'final_norm': gain(ks[3], (D_MODEL,)),
        'hgrn_w_in': dense(ks[4], (nh, D_MODEL, HGRN_IN_DIM)),
        'hgrn_lb_logits': 0.5 * jax.random.normal(ks[5], (nh, HGRN_FDIM), f32),
        'hgrn_gnorm': gain(ks[6], (nh, HGRN_DV)),
        'hgrn_w_out': dense(ks[7], (nh, HGRN_HEADS * HGRN_DV, D_MODEL)),
        'm_w_in': dense(ks[8], (nm, D_MODEL, M_IN_DIM)),
        'm_conv_w': dense(ks[9], (nm, M_CONV, M_CONV_DIM)),
        'm_conv_b': small(ks[11], (nm, M_CONV_DIM)),
        'm_dt_bias': dt0 + jnp.log(-jnp.expm1(-dt0)),
        'm_A_log': jnp.log(jax.random.uniform(ks[12], (nm, M_HEADS), f32, 1.0, 16.0)),
        'm_D': gain(ks[13], (nm, M_HEADS)),
        'm_norm': gain(ks[14], (nm, M_D_INNER)),
        'm_w_out': dense(ks[15], (nm, M_D_INNER, D_MODEL)),
        'f_w_up': dense(ks[16], (DEPTH, D_MODEL, 2 * D_FF)),
        'f_conv_w': dense(ks[17], (DEPTH, FFN_CONV, 2 * D_FF)),
        'f_conv_b': small(ks[18], (DEPTH, 2 * D_FF)),
        'f_w_down': dense(ks[19], (DEPTH, D_FF, D_MODEL)),
    }


def reference(x, mix_norm, ffn_norm, final_norm, hgrn_w_in, hgrn_lb_logits, hgrn_gnorm, hgrn_w_out,
              m_w_in, m_conv_w, m_conv_b, m_dt_bias, m_A_log, m_D, m_norm, m_w_out,
              f_w_up, f_conv_w, f_conv_b, f_w_down):
    lb_p = jax.nn.softmax(hgrn_lb_logits.astype(jnp.float32), axis=0)
    lower_bounds = jnp.cumsum(lb_p, axis=0) - lb_p[0]
    h = x
    for i in range(DEPTH):
        u = rms_norm(h, mix_norm[i])
        j = i // N_MIXERS
        if i % N_MIXERS == 0:
            h = h + hgrn2_mixer(u, hgrn_w_in[j], lower_bounds[j], hgrn_gnorm[j], hgrn_w_out[j])
        else:
            h = h + mamba2_mixer(u, m_w_in[j], m_conv_w[j], m_conv_b[j], m_dt_bias[j], m_A_log[j],
                                 m_D[j], m_norm[j], m_w_out[j])
        h = h + conv_ffn(rms_norm(h, ffn_norm[i]), f_w_up[i], f_conv_w[i], f_conv_b[i], f_w_down[i])
    return rms_norm(h, final_norm)
```

```python
import functools

import jax
import jax.numpy as jnp
from jax import lax
from jax.experimental import pallas as pl
from jax.experimental.pallas import tpu as pltpu

F32 = jnp.float32
BF16 = jnp.bfloat16

NORM_EPS = 1e-5
LANES = 128
HGRN_EXPAND = 128
HGRN_CHUNK = 64
HGRN_SUB = 16
M_HEADDIM = 64
M_GROUPS = 8
M_D_STATE = 128
M_CONV = 4
M_CHUNK = 128
FFN_CONV = 3
VMEM_LIMIT = 56 * 1024 * 1024


def _cparams(sem):
    return pltpu.CompilerParams(dimension_semantics=sem, vmem_limit_bytes=VMEM_LIMIT)


def _pick(n, pref):
    t = min(pref, n)
    while n % t:
        t -= LANES if t > LANES else 8
    return t


def _dot(a, b):
    return jnp.dot(a, b, preferred_element_type=F32)


def _dot_nt(a, b):
    return lax.dot_general(a, b, (((1,), (1,)), ((), ())), preferred_element_type=F32)


def _dot_tn(a, b):
    return lax.dot_general(a, b, (((0,), (0,)), ((), ())), preferred_element_type=F32)


def _split3(x):
    hi = x.astype(BF16)
    r1 = x - hi.astype(F32)
    mid = r1.astype(BF16)
    lo = (r1 - mid.astype(F32)).astype(BF16)
    return hi, mid, lo


def _cumsum_rows(x, tri):
    hi, mid, lo = _split3(x)
    return _dot(tri, hi) + _dot(tri, mid) + _dot(tri, lo)


def _tri(n, dtype):
    r = lax.broadcasted_iota(jnp.int32, (n, n), 0)
    c = lax.broadcasted_iota(jnp.int32, (n, n), 1)
    return (r >= c).astype(dtype)


def _sigmoid(x):
    return 1.0 / (1.0 + jnp.exp(-x))


def _silu(x):
    return x * _sigmoid(x)


def _softplus(x):
    return jnp.maximum(x, 0.0) + jnp.log(1.0 + jnp.exp(-jnp.abs(x)))


def _rmsnorm_kernel(x_ref, w_ref, o_ref):
    x = x_ref[...]
    y = x * lax.rsqrt(jnp.mean(x * x, axis=-1, keepdims=True) + NORM_EPS)
    o_ref[...] = (y * w_ref[...]).astype(o_ref.dtype)


def rmsnorm(x, w, out_dtype):
    t, d = x.shape
    tm = _pick(t, 512)
    return pl.pallas_call(
        _rmsnorm_kernel,
        grid=(t // tm,),
        in_specs=[pl.BlockSpec((tm, d), lambda i: (i, 0)), pl.BlockSpec((1, d), lambda i: (0, 0))],
        out_specs=pl.BlockSpec((tm, d), lambda i: (i, 0)),
        out_shape=jax.ShapeDtypeStruct((t, d), out_dtype),
        compiler_params=_cparams(("parallel",)),
        name="rmsnorm",
    )(x, w.reshape(1, d))


def _mm_kernel(x_ref, w_ref, o_ref):
    o_ref[...] = _dot(x_ref[...], w_ref[...]).astype(o_ref.dtype)


def matmul(x, w, out_dtype=F32, tm=1024, tn=1024):
    t, k = x.shape
    n = w.shape[1]
    tm, tn = _pick(t, tm), _pick(n, tn)
    return pl.pallas_call(
        _mm_kernel,
        grid=(t // tm, n // tn),
        in_specs=[pl.BlockSpec((tm, k), lambda i, j: (i, 0)), pl.BlockSpec((k, tn), lambda i, j: (0, j))],
        out_specs=pl.BlockSpec((tm, tn), lambda i, j: (i, j)),
        out_shape=jax.ShapeDtypeStruct((t, n), out_dtype),
        compiler_params=_cparams(("parallel", "parallel")),
        name="matmul",
    )(x, w)


def _mm_heads_kernel(x_ref, w_ref, o_ref, *, nblk):
    acc = _dot(x_ref[...], w_ref[...])
    for c in range(nblk):
        o_ref[c] = acc[:, c * LANES:(c + 1) * LANES]


def matmul_headmajor(x, w, tm=1024, tn=1024):
    t, k = x.shape
    n = w.shape[1]
    tm, tn = _pick(t, tm), _pick(n, tn)
    nblk = tn // LANES
    return pl.pallas_call(
        functools.partial(_mm_heads_kernel, nblk=nblk),
        grid=(t // tm, n // tn),
        in_specs=[pl.BlockSpec((tm, k), lambda i, j: (i, 0)), pl.BlockSpec((k, tn), lambda i, j: (0, j))],
        out_specs=pl.BlockSpec((nblk, tm, LANES), lambda i, j: (j, i, 0)),
        out_shape=jax.ShapeDtypeStruct((n // LANES, t, LANES), F32),
        compiler_params=_cparams(("parallel", "parallel")),
        name="matmul_headmajor",
    )(x, w)


def _mm_res_norm_kernel(x_ref, w_ref, h_ref, nw_ref, h_out_ref, u_out_ref, acc_ref, *, nk):
    k = pl.program_id(1)

    @pl.when(k == 0)
    def _():
        acc_ref[...] = h_ref[...]

    acc_ref[...] += _dot(x_ref[...], w_ref[...])

    @pl.when(k == nk - 1)
    def _():
        h = acc_ref[...]
        h_out_ref[...] = h
        y = h * lax.rsqrt(jnp.mean(h * h, axis=-1, keepdims=True) + NORM_EPS)
        u_out_ref[...] = (y * nw_ref[...]).astype(u_out_ref.dtype)


def matmul_res_norm(x, w, h, norm_w, u_dtype, tm=512, tk=1408):
    t, kdim = x.shape
    d = w.shape[1]
    tm = _pick(t, tm)
    tk = _pick(kdim, tk)
    nk = kdim // tk
    return pl.pallas_call(
        functools.partial(_mm_res_norm_kernel, nk=nk),
        grid=(t // tm, nk),
        in_specs=[
            pl.BlockSpec((tm, tk), lambda i, k: (i, k)),
            pl.BlockSpec((tk, d), lambda i, k: (k, 0)),
            pl.BlockSpec((tm, d), lambda i, k: (i, 0)),
            pl.BlockSpec((1, d), lambda i, k: (0, 0)),
        ],
        out_specs=[pl.BlockSpec((tm, d), lambda i, k: (i, 0)), pl.BlockSpec((tm, d), lambda i, k: (i, 0))],
        out_shape=[jax.ShapeDtypeStruct((t, d), F32), jax.ShapeDtypeStruct((t, d), u_dtype)],
        scratch_shapes=[pltpu.VMEM((tm, d), F32)],
        compiler_params=_cparams(("parallel", "arbitrary")),
        name="matmul_res_norm",
    )(x, w, h, norm_w.reshape(1, d))


def _shift_rows(x, halo, s):
    rolled = pltpu.roll(x, s, axis=0)
    row = lax.broadcasted_iota(jnp.int32, halo.shape, 0)
    top = jnp.where(row < s, pltpu.roll(halo, s, axis=0), rolled[:8])
    return jnp.concatenate([top, rolled[8:]], axis=0)


def _ffn_up_kernel(u_ref, uh_ref, wg_ref, wu_ref, cwg_ref, cwu_ref, cbg_ref, cbu_ref, o_ref, *, tiles_per_seq):
    i = pl.program_id(0)
    first = (i % tiles_per_seq) == 0
    u = u_ref[...]
    uh = jnp.where(first, jnp.zeros_like(uh_ref[...]), uh_ref[...])

    def conv(w_ref, cw_ref, cb_ref):
        x = _dot(u, w_ref[...])
        xh = _dot(uh, w_ref[...])[8:]
        cw = cw_ref[...]
        return (cb_ref[...] + cw[2:3] * x + cw[1:2] * _shift_rows(x, xh, 1) + cw[0:1] * _shift_rows(x, xh, 2))

    g = conv(wg_ref, cwg_ref, cbg_ref)
    up = conv(wu_ref, cwu_ref, cbu_ref)
    o_ref[...] = (_silu(g) * up).astype(o_ref.dtype)


def ffn_up(u, w_up, conv_w, conv_b, seq_len, tm=1024, tn=512):
    t, d = u.shape
    dff = w_up.shape[1] // 2
    tm = _pick(seq_len, tm)
    tn = _pick(dff, tn)
    nj = dff // tn
    tm16 = tm // 16
    return pl.pallas_call(
        functools.partial(_ffn_up_kernel, tiles_per_seq=seq_len // tm),
        grid=(t // tm, nj),
        in_specs=[
            pl.BlockSpec((tm, d), lambda i, j: (i, 0)),
            pl.BlockSpec((16, d), lambda i, j: (jnp.maximum(i * tm16 - 1, 0), 0)),
            pl.BlockSpec((d, tn), lambda i, j: (0, j)),
            pl.BlockSpec((d, tn), lambda i, j: (0, j + nj)),
            pl.BlockSpec((FFN_CONV, tn), lambda i, j: (0, j)),
            pl.BlockSpec((FFN_CONV, tn), lambda i, j: (0, j + nj)),
            pl.BlockSpec((1, tn), lambda i, j: (0, j)),
            pl.BlockSpec((1, tn), lambda i, j: (0, j + nj)),
        ],
        out_specs=pl.BlockSpec((tm, tn), lambda i, j: (i, j)),
        out_shape=jax.ShapeDtypeStruct((t, dff), BF16),
        compiler_params=_cparams(("parallel", "parallel")),
        name="ffn_up",
    )(u, u, w_up, w_up, conv_w, conv_w, conv_b.reshape(1, -1), conv_b.reshape(1, -1))


def _hgrn_kernel(q_ref, f_ref, v_ref, g_ref, lb_ref, gn_ref, sel_ref, o_ref, st_ref, b_ref, k_ref, *, n_chunks):
    C, S = HGRN_CHUNK, HGRN_SUB
    nsub = C // S
    dk = q_ref.shape[-1]
    lb = lb_ref[0]
    one_m_lb = 1.0 - lb
    gn = gn_ref[...]
    tri = _tri(C, BF16)
    row = lax.broadcasted_iota(jnp.int32, (C, C), 0)
    col = lax.broadcasted_iota(jnp.int32, (C, C), 1)
    mask_off = (col // S) < (row // S)
    mask_diag = ((col // S) == (row // S)) & (col <= row)

    st_ref[...] = jnp.zeros_like(st_ref)

    def chunk(ci, carry):
        r0 = pl.multiple_of(ci * C, C)
        q = _silu(q_ref[0, pl.ds(r0, C), :])
        sg = _sigmoid(f_ref[0, pl.ds(r0, C), :])
        v = v_ref[0, pl.ds(r0, C), :]
        logf = jnp.log(lb + one_m_lb * sg)
        k = one_m_lb * (1.0 - sg)
        b = _cumsum_rows(logf, tri)
        b_ref[...] = b
        k_ref[...] = k
        vb = v.astype(BF16)

        refs = [jnp.zeros((1, dk), F32)] + [b_ref[pl.ds(S * i - 1, 1), :] for i in range(1, nsub)]
        bref = jnp.concatenate([jnp.broadcast_to(r, (S, dk)) for r in refs], axis=0)
        qs = (q * jnp.exp(b - bref)).astype(BF16)

        blocks = [jnp.zeros((S, C), F32)]
        for i in range(1, nsub):
            kt = (k * jnp.exp(jnp.minimum(refs[i] - b, 0.0))).astype(BF16)
            blocks.append(_dot_nt(qs[S * i:S * (i + 1)], kt))
        a_off = jnp.concatenate(blocks, axis=0)

        pieces = []
        for jj in range(S):
            bj = jnp.concatenate(
                [jnp.broadcast_to(b_ref[pl.ds(S * i + jj, 1), :], (S, dk)) for i in range(nsub)], axis=0)
            kj = jnp.concatenate(
                [jnp.broadcast_to(k_ref[pl.ds(S * i + jj, 1), :], (S, dk)) for i in range(nsub)], axis=0)
            pieces.append((q * jnp.exp(jnp.minimum(b - bj, 0.0)) * kj).astype(BF16))
        a_diag = _dot(jnp.concatenate(pieces, axis=1), sel_ref[...])[:, :C]

        a = jnp.where(mask_off, a_off, 0.0) + jnp.where(mask_diag, a_diag, 0.0)

        st = st_ref[...]
        qe = (q * jnp.exp(b)).astype(BF16)
        o = _dot(a.astype(BF16), vb) + _dot_nt(qe, st.astype(BF16))

        b_last = b_ref[pl.ds(C - 1, 1), :]
        kd = (k * jnp.exp(b_last - b)).astype(BF16)
        st_ref[...] = st * jnp.exp(b_last) + _dot_tn(vb, kd)

        gate = _silu(g_ref[0, pl.ds(r0, C), :])
        y = o * lax.rsqrt(jnp.mean(o * o, axis=-1, keepdims=True) + NORM_EPS) * gn * gate
        o_ref[pl.ds(r0, C), :] = y.astype(o_ref.dtype)
        return carry

    lax.fori_loop(0, n_chunks, chunk, 0)


def _hgrn_selector(dk):
    jj = jnp.arange(HGRN_SUB * dk) // dk
    c = jnp.arange(LANES)
    return ((c[None, :] % HGRN_SUB == jj[:, None]) & (c[None, :] < HGRN_CHUNK)).astype(BF16)


def hgrn_core(proj, lb, gn_w, batch, seq_len):
    nblk, t, dk = proj.shape
    heads = nblk // 4
    n_chunks = seq_len // HGRN_CHUNK

    def spec(section):
        return pl.BlockSpec((1, seq_len, dk), lambda b, h: (section * heads + h, b, 0))

    return pl.pallas_call(
        functools.partial(_hgrn_kernel, n_chunks=n_chunks),
        grid=(batch, heads),
        in_specs=[
            spec(0), spec(1), spec(2), spec(3),
            pl.BlockSpec((1, 1, dk), lambda b, h: (h, 0, 0)),
            pl.BlockSpec((1, dk), lambda b, h: (0, 0)),
            pl.BlockSpec((HGRN_SUB * dk, LANES), lambda b, h: (0, 0)),
        ],
        out_specs=pl.BlockSpec((seq_len, dk), lambda b, h: (b, h)),
        out_shape=jax.ShapeDtypeStruct((t, heads * dk), BF16),
        scratch_shapes=[
            pltpu.VMEM((dk, dk), F32),
            pltpu.VMEM((HGRN_CHUNK, dk), F32),
            pltpu.VMEM((HGRN_CHUNK, dk), F32),
        ],
        compiler_params=_cparams(("parallel", "parallel")),
        name="hgrn_core",
    )(proj, proj, proj, proj, lb.reshape(heads, 1, dk), gn_w.reshape(1, dk), _hgrn_selector(dk))


def _conv_silu(x_ref, halo_ref, w_ref, b_ref, dst_ref, first, slab):
    width = x_ref.shape[-1]

    def body(s, carry):
        c0 = pl.multiple_of(s * slab, slab)
        x = x_ref[:, pl.ds(c0, slab)]
        halo = jnp.where(first, 0.0, halo_ref[:, pl.ds(c0, slab)])
        w = w_ref[:, pl.ds(c0, slab)]
        acc = b_ref[:, pl.ds(c0, slab)] + w[M_CONV - 1:M_CONV] * x
        for sh in range(1, M_CONV):
            acc = acc + w[M_CONV - 1 - sh:M_CONV - sh] * _shift_rows(x, halo, sh)
        dst_ref[:, pl.ds(c0, slab)] = _silu(acc)
        return carry

    lax.fori_loop(0, width // slab, body, 0)


def _ssd_kernel(z_ref, x_ref, bm_ref, cm_ref, xh_ref, bh_ref, ch_ref, dt_ref,
                cwx_ref, cwb_ref, cwc_ref, cbx_ref, cbb_ref, cbc_ref,
                dtb_ref, alog_ref, dskip_ref, nw_ref, o_ref,
                xs_ref, bs_ref, cs_ref, y_ref, hst_ref, acs_ref, acst_ref, dtt_ref, *, hpg):
    L = M_CHUNK
    n = M_D_STATE
    first = pl.program_id(1) == 0

    @pl.when(first)
    def _():
        hst_ref[...] = jnp.zeros_like(hst_ref)

    _conv_silu(x_ref, xh_ref, cwx_ref, cbx_ref, xs_ref, first, 512)
    _conv_silu(bm_ref, bh_ref, cwb_ref, cbb_ref, bs_ref, first, 512)
    _conv_silu(cm_ref, ch_ref, cwc_ref, cbc_ref, cs_ref, first, 512)

    dt = _softplus(dt_ref[...] + dtb_ref[...])
    a_neg = -jnp.exp(alog_ref[...])
    acs = _cumsum_rows(dt * a_neg, _tri(L, BF16))
    acs_ref[...] = acs
    acst_ref[...] = acs.T
    dtt_ref[...] = dt.T

    row = lax.broadcasted_iota(jnp.int32, (L, L), 0)
    col = lax.broadcasted_iota(jnp.int32, (L, L), 1)
    causal = row >= col
    lane = lax.broadcasted_iota(jnp.int32, (L, LANES), 1)
    left = lane < M_HEADDIM
    npairs = hpg // 2

    def group(g, carry):
        g0 = pl.multiple_of(g * n, n)
        bg = bs_ref[:, pl.ds(g0, n)]
        cg = cs_ref[:, pl.ds(g0, n)]
        cb = _dot_nt(cg.astype(BF16), bg.astype(BF16))
        bgt = bg.T
        for p in range(npairs):
            x0 = pl.multiple_of((g * npairs + p) * LANES, LANES)
            xp = xs_ref[:, pl.ds(x0, LANES)].astype(BF16)
            hp = hst_ref[:, pl.ds(x0, LANES)]
            rhs = jnp.concatenate([xp, hp.astype(BF16)], axis=0)
            ys, hs = [], []
            for j in range(2):
                h = g * hpg + 2 * p + j
                a_row = acst_ref[pl.ds(h, 1), :]
                dt_row = dtt_ref[pl.ds(h, 1), :]
                a_col = jnp.sum(jnp.where(lane == h, acs_ref[...], 0.0), axis=1, keepdims=True)
                lmat = jnp.where(causal, jnp.exp(jnp.minimum(a_col - a_row, 0.0)), 0.0)
                m = (cb * lmat * dt_row).astype(BF16)
                ce = (cg * jnp.exp(a_col)).astype(BF16)
                ys.append(_dot(jnp.concatenate([m, ce], axis=1), rhs))
                a_last = a_row[:, L - 1:L]
                w_s = jnp.exp(a_last - a_row) * dt_row
                ds = _dot((bgt * w_s).astype(BF16), xp)
                hs.append(jnp.exp(a_last) * hp + ds)
            y_ref[:, pl.ds(x0, LANES)] = jnp.where(left, ys[0], ys[1])
            hst_ref[:, pl.ds(x0, LANES)] = jnp.where(left, hs[0], hs[1])
        return carry

    lax.fori_loop(0, M_GROUPS, group, 0)

    gw = hpg * M_HEADDIM

    def finish(g, carry):
        c0 = pl.multiple_of(g * gw, gw)
        y = y_ref[:, pl.ds(c0, gw)] + xs_ref[:, pl.ds(c0, gw)] * dskip_ref[:, pl.ds(c0, gw)]
        y = y * _silu(z_ref[:, pl.ds(c0, gw)])
        y = y * lax.rsqrt(jnp.mean(y * y, axis=-1, keepdims=True) + NORM_EPS)
        o_ref[:, pl.ds(c0, gw)] = (y * nw_ref[:, pl.ds(c0, gw)]).astype(o_ref.dtype)
        return carry

    lax.fori_loop(0, M_GROUPS, finish, 0)


def ssd_core(zx, dt, conv_w, conv_b, dt_bias, a_log, d_skip, norm_w, batch, seq_len):
    t = zx.shape[0]
    gn = M_GROUPS * M_D_STATE
    d_inner = (zx.shape[1] - 2 * gn) // 2
    heads = d_inner // M_HEADDIM
    hpg = heads // M_GROUPS
    L = M_CHUNK
    nc = seq_len // L
    assert d_inner % gn == 0 and heads <= LANES and hpg % 2 == 0
    r = d_inner // gn
    l8 = L // 8

    def rows(b, c):
        return b * nc + c

    def halo_rows(b, c):
        return jnp.maximum((b * nc + c) * l8 - 1, 0)

    pad = LANES - heads
    dtb = jnp.pad(dt_bias.astype(F32), (0, pad)).reshape(1, LANES)
    alog = jnp.pad(a_log.astype(F32), (0, pad)).reshape(1, LANES)
    dskip = jnp.repeat(d_skip.astype(F32), M_HEADDIM).reshape(1, d_inner)
    cb2 = conv_b.reshape(1, -1)

    in_specs = [
        pl.BlockSpec((L, d_inner), lambda b, c: (rows(b, c), 0)),
        pl.BlockSpec((L, d_inner), lambda b, c: (rows(b, c), 1)),
        pl.BlockSpec((L, gn), lambda b, c: (rows(b, c), 2 * r)),
        pl.BlockSpec((L, gn), lambda b, c: (rows(b, c), 2 * r + 1)),
        pl.BlockSpec((8, d_inner), lambda b, c: (halo_rows(b, c), 1)),
        pl.BlockSpec((8, gn), lambda b, c: (halo_rows(b, c), 2 * r)),
        pl.BlockSpec((8, gn), lambda b, c: (halo_rows(b, c), 2 * r + 1)),
        pl.BlockSpec((L, LANES), lambda b, c: (rows(b, c), 0)),
        pl.BlockSpec((M_CONV, d_inner), lambda b, c: (0, 0)),
        pl.BlockSpec((M_CONV, gn), lambda b, c: (0, r)),
        pl.BlockSpec((M_CONV, gn), lambda b, c: (0, r + 1)),
        pl.BlockSpec((1, d_inner), lambda b, c: (0, 0)),
        pl.BlockSpec((1, gn), lambda b, c: (0, r)),
        pl.BlockSpec((1, gn), lambda b, c: (0, r + 1)),
        pl.BlockSpec((1, LANES), lambda b, c: (0, 0)),
        pl.BlockSpec((1, LANES), lambda b, c: (0, 0)),
        pl.BlockSpec((1, d_inner), lambda b, c: (0, 0)),
        pl.BlockSpec((1, d_inner), lambda b, c: (0, 0)),
    ]
    return pl.pallas_call(
        functools.partial(_ssd_kernel, hpg=hpg),
        grid=(batch, nc),
        in_specs=in_specs,
        out_specs=pl.BlockSpec((L, d_inner), lambda b, c: (rows(b, c), 0)),
        out_shape=jax.ShapeDtypeStruct((t, d_inner), BF16),
        scratch_shapes=[
            pltpu.VMEM((L, d_inner), F32),
            pltpu.VMEM((L, gn), F32),
            pltpu.VMEM((L, gn), F32),
            pltpu.VMEM((L, d_inner), F32),
            pltpu.VMEM((M_D_STATE, d_inner), F32),
            pltpu.VMEM((L, LANES), F32),
            pltpu.VMEM((LANES, L), F32),
            pltpu.VMEM((LANES, L), F32),
        ],
        compiler_params=_cparams(("parallel", "arbitrary")),
        name="ssd_core",
    )(zx, zx, zx, zx, zx, zx, zx, dt, conv_w, conv_w, conv_w, cb2, cb2, cb2, dtb, alog, dskip,
      norm_w.reshape(1, d_inner))


def kernel(x, mix_norm, ffn_norm, final_norm, hgrn_w_in, hgrn_lb_logits, hgrn_gnorm, hgrn_w_out,
           m_w_in, m_conv_w, m_conv_b, m_dt_bias, m_A_log, m_D, m_norm, m_w_out,
           f_w_up, f_conv_w, f_conv_b, f_w_down):
    batch, seq_len, d = x.shape
    depth = mix_norm.shape[0]
    t = batch * seq_len

    lb_p = jax.nn.softmax(hgrn_lb_logits.astype(F32), axis=0)
    lower_bounds = jnp.cumsum(lb_p, axis=0) - lb_p[0]

    d_inner = m_w_out.shape[1]
    m_heads = d_inner // M_HEADDIM
    zx_dim = m_w_in.shape[2] - m_heads

    h = x.reshape(t, d)
    u = rmsnorm(h, mix_norm[0], BF16)
    out = None
    for i in range(depth):
        j = i // 2
        if i % 2 == 0:
            proj = matmul_headmajor(u, hgrn_w_in[j].astype(BF16))
            y = hgrn_core(proj, lower_bounds[j], hgrn_gnorm[j], batch, seq_len)
            w_out = hgrn_w_out[j]
        else:
            w_in = m_w_in[j]
            zx = matmul(u, w_in[:, :zx_dim].astype(BF16))
            w_dt = jnp.pad(w_in[:, zx_dim:], ((0, 0), (0, LANES - m_heads))).astype(BF16)
            dt = matmul(u, w_dt, tn=LANES)
            y = ssd_core(zx, dt, m_conv_w[j], m_conv_b[j], m_dt_bias[j], m_A_log[j], m_D[j], m_norm[j],
                         batch, seq_len)
            w_out = m_w_out[j]
        h, u = matmul_res_norm(y, w_out.astype(BF16), h, ffn_norm[i], BF16)
        a = ffn_up(u, f_w_up[i].astype(BF16), f_conv_w[i], f_conv_b[i], seq_len)
        if i + 1 < depth:
            h, u = matmul_res_norm(a, f_w_down[i].astype(BF16), h, mix_norm[i + 1], BF16)
        else:
            h, out = matmul_res_norm(a, f_w_down[i].astype(BF16), h, final_norm, F32)
    return out.reshape(batch, seq_len, d)
```

```python
import functools

import jax
import jax.numpy as jnp
from jax import lax
from jax.experimental import pallas as pl
from jax.experimental.pallas import tpu as pltpu

F32 = jnp.float32
BF16 = jnp.bfloat16

NORM_EPS = 1e-5
LANES = 128
HGRN_EXPAND = 128
HGRN_CHUNK = 64
HGRN_SUB = 16
M_HEADDIM = 64
M_GROUPS = 8
M_D_STATE = 128
M_CONV = 4
M_CHUNK = 128
FFN_CONV = 3
VMEM_LIMIT = 56 * 1024 * 1024


def _cparams(sem):
    return pltpu.CompilerParams(dimension_semantics=sem, vmem_limit_bytes=VMEM_LIMIT)


def _pick(n, pref):
    t = min(pref, n)
    while n % t:
        t -= LANES if t > LANES else 8
    return t


def _dot(a, b):
    return jnp.dot(a, b, preferred_element_type=F32)


def _dot_nt(a, b):
    return lax.dot_general(a, b, (((1,), (1,)), ((), ())), preferred_element_type=F32)


def _dot_tn(a, b):
    return lax.dot_general(a, b, (((0,), (0,)), ((), ())), preferred_element_type=F32)


def _split3(x):
    hi = x.astype(BF16)
    r1 = x - hi.astype(F32)
    mid = r1.astype(BF16)
    lo = (r1 - mid.astype(F32)).astype(BF16)
    return hi, mid, lo


def _cumsum_rows(x, tri):
    hi, mid, lo = _split3(x)
    return _dot(tri, hi) + _dot(tri, mid) + _dot(tri, lo)


def _tri(n, dtype):
    r = lax.broadcasted_iota(jnp.int32, (n, n), 0)
    c = lax.broadcasted_iota(jnp.int32, (n, n), 1)
    return (r >= c).astype(dtype)


def _sigmoid(x):
    return 1.0 / (1.0 + jnp.exp(-x))


def _silu(x):
    return x * _sigmoid(x)


def _softplus(x):
    return jnp.maximum(x, 0.0) + jnp.log(1.0 + jnp.exp(-jnp.abs(x)))


def _rmsnorm_kernel(x_ref, w_ref, o_ref):
    x = x_ref[...]
    y = x * lax.rsqrt(jnp.mean(x * x, axis=-1, keepdims=True) + NORM_EPS)
    o_ref[...] = (y * w_ref[...]).astype(o_ref.dtype)


def rmsnorm(x, w, out_dtype):
    t, d = x.shape
    tm = _pick(t, 512)
    return pl.pallas_call(
        _rmsnorm_kernel,
        grid=(t // tm,),
        in_specs=[pl.BlockSpec((tm, d), lambda i: (i, 0)), pl.BlockSpec((1, d), lambda i: (0, 0))],
        out_specs=pl.BlockSpec((tm, d), lambda i: (i, 0)),
        out_shape=jax.ShapeDtypeStruct((t, d), out_dtype),
        compiler_params=_cparams(("parallel",)),
        name="rmsnorm",
    )(x, w.reshape(1, d))


def _mm_kernel(x_ref, w_ref, o_ref):
    o_ref[...] = _dot(x_ref[...], w_ref[...]).astype(o_ref.dtype)


def matmul(x, w, out_dtype=F32, tm=2048, tn=1024):
    t, k = x.shape
    n = w.shape[1]
    tm, tn = _pick(t, tm), _pick(n, tn)
    return pl.pallas_call(
        _mm_kernel,
        grid=(t // tm, n // tn),
        in_specs=[pl.BlockSpec((tm, k), lambda i, j: (i, 0)), pl.BlockSpec((k, tn), lambda i, j: (0, j))],
        out_specs=pl.BlockSpec((tm, tn), lambda i, j: (i, j)),
        out_shape=jax.ShapeDtypeStruct((t, n), out_dtype),
        compiler_params=_cparams(("parallel", "parallel")),
        name="matmul",
    )(x, w)


def _mm_heads_kernel(x_ref, w_ref, o_ref, *, nblk):
    acc = _dot(x_ref[...], w_ref[...])
    for c in range(nblk):
        o_ref[c] = acc[:, c * LANES:(c + 1) * LANES]


def matmul_headmajor(x, w, tm=2048, tn=1024):
    t, k = x.shape
    n = w.shape[1]
    tm, tn = _pick(t, tm), _pick(n, tn)
    nblk = tn // LANES
    return pl.pallas_call(
        functools.partial(_mm_heads_kernel, nblk=nblk),
        grid=(t // tm, n // tn),
        in_specs=[pl.BlockSpec((tm, k), lambda i, j: (i, 0)), pl.BlockSpec((k, tn), lambda i, j: (0, j))],
        out_specs=pl.BlockSpec((nblk, tm, LANES), lambda i, j: (j, i, 0)),
        out_shape=jax.ShapeDtypeStruct((n // LANES, t, LANES), F32),
        compiler_params=_cparams(("parallel", "parallel")),
        name="matmul_headmajor",
    )(x, w)


def _mm_res_norm_kernel(x_ref, w_ref, h_ref, nw_ref, h_out_ref, u_out_ref):
    h = h_ref[...] + _dot(x_ref[...], w_ref[...])
    h_out_ref[...] = h
    y = h * lax.rsqrt(jnp.mean(h * h, axis=-1, keepdims=True) + NORM_EPS)
    u_out_ref[...] = (y * nw_ref[...]).astype(u_out_ref.dtype)


def matmul_res_norm(x, w, h, norm_w, u_dtype):
    t, kdim = x.shape
    d = w.shape[1]
    row_bytes = 2 * 2 * kdim + (4 * 4 + 2 * jnp.dtype(u_dtype).itemsize + 4) * d
    budget = VMEM_LIMIT - 2 * kdim * d - (6 << 20)
    tm = 1024
    while tm * row_bytes > budget or t % tm:
        tm //= 2
    return pl.pallas_call(
        _mm_res_norm_kernel,
        grid=(t // tm,),
        in_specs=[
            pl.BlockSpec((tm, kdim), lambda i: (i, 0)),
            pl.BlockSpec((kdim, d), lambda i: (0, 0), pipeline_mode=pl.Buffered(1)),
            pl.BlockSpec((tm, d), lambda i: (i, 0)),
            pl.BlockSpec((1, d), lambda i: (0, 0)),
        ],
        out_specs=[pl.BlockSpec((tm, d), lambda i: (i, 0)), pl.BlockSpec((tm, d), lambda i: (i, 0))],
        out_shape=[jax.ShapeDtypeStruct((t, d), F32), jax.ShapeDtypeStruct((t, d), u_dtype)],
        compiler_params=_cparams(("parallel",)),
        name="matmul_res_norm",
    )(x, w, h, norm_w.reshape(1, d))


def _shift_rows(x, halo, s):
    rolled = pltpu.roll(x, s, axis=0)
    row = lax.broadcasted_iota(jnp.int32, halo.shape, 0)
    top = jnp.where(row < s, pltpu.roll(halo, s, axis=0), rolled[:8])
    return jnp.concatenate([top, rolled[8:]], axis=0)


def _ffn_up_kernel(u_ref, uh_ref, wg_ref, wu_ref, cwg_ref, cwu_ref, cbg_ref, cbu_ref, o_ref, *, tiles_per_seq):
    i = pl.program_id(0)
    first = (i % tiles_per_seq) == 0
    u = u_ref[...]
    uh = jnp.where(first, jnp.zeros_like(uh_ref[...]), uh_ref[...])

    def conv(w_ref, cw_ref, cb_ref):
        x = _dot(u, w_ref[...])
        xh = _dot(uh, w_ref[...])[8:]
        cw = cw_ref[...]
        return (cb_ref[...] + cw[2:3] * x + cw[1:2] * _shift_rows(x, xh, 1) + cw[0:1] * _shift_rows(x, xh, 2))

    g = conv(wg_ref, cwg_ref, cbg_ref)
    up = conv(wu_ref, cwu_ref, cbu_ref)
    o_ref[...] = (_silu(g) * up).astype(o_ref.dtype)


def ffn_up(u, w_up, conv_w, conv_b, seq_len, tm=1024, tn=512):
    t, d = u.shape
    dff = w_up.shape[1] // 2
    tm = _pick(seq_len, tm)
    tn = _pick(dff, tn)
    nj = dff // tn
    tm16 = tm // 16
    return pl.pallas_call(
        functools.partial(_ffn_up_kernel, tiles_per_seq=seq_len // tm),
        grid=(t // tm, nj),
        in_specs=[
            pl.BlockSpec((tm, d), lambda i, j: (i, 0)),
            pl.BlockSpec((16, d), lambda i, j: (jnp.maximum(i * tm16 - 1, 0), 0)),
            pl.BlockSpec((d, tn), lambda i, j: (0, j)),
            pl.BlockSpec((d, tn), lambda i, j: (0, j + nj)),
            pl.BlockSpec((FFN_CONV, tn), lambda i, j: (0, j)),
            pl.BlockSpec((FFN_CONV, tn), lambda i, j: (0, j + nj)),
            pl.BlockSpec((1, tn), lambda i, j: (0, j)),
            pl.BlockSpec((1, tn), lambda i, j: (0, j + nj)),
        ],
        out_specs=pl.BlockSpec((tm, tn), lambda i, j: (i, j)),
        out_shape=jax.ShapeDtypeStruct((t, dff), BF16),
        compiler_params=_cparams(("parallel", "parallel")),
        name="ffn_up",
    )(u, u, w_up, w_up, conv_w, conv_w, conv_b.reshape(1, -1), conv_b.reshape(1, -1))


def _hgrn_kernel(q_ref, f_ref, v_ref, g_ref, lb_ref, gn_ref, sel_ref, o_ref, st_ref, q_s, k_s, b_s, *, n_chunks):
    C, S = HGRN_CHUNK, HGRN_SUB
    nsub = C // S
    dk = q_ref.shape[-1]
    lb = lb_ref[0]
    one_m_lb = 1.0 - lb
    gn = gn_ref[...]
    tri = _tri(C, BF16)
    row = lax.broadcasted_iota(jnp.int32, (C, C), 0)
    col = lax.broadcasted_iota(jnp.int32, (C, C), 1)
    mask_diag = ((col // S) == (row // S)) & (col <= row)

    def prepare(ci, carry):
        r0 = pl.multiple_of(ci * C, C)
        sg = _sigmoid(f_ref[0, pl.ds(r0, C), :])
        q_s[pl.ds(r0, C), :] = _silu(q_ref[0, pl.ds(r0, C), :])
        k_s[pl.ds(r0, C), :] = one_m_lb * (1.0 - sg)
        b_s[pl.ds(r0, C), :] = _cumsum_rows(jnp.log2(lb + one_m_lb * sg), tri)
        return carry

    lax.fori_loop(0, n_chunks, prepare, 0, unroll=4)

    st_ref[...] = jnp.zeros_like(st_ref)

    def chunk(ci, carry):
        r0 = pl.multiple_of(ci * C, C)
        q = q_s[pl.ds(r0, C), :]
        k = k_s[pl.ds(r0, C), :]
        b = b_s[pl.ds(r0, C), :]
        vb = v_ref[0, pl.ds(r0, C), :].astype(BF16)

        def brow(ref, r):
            return jnp.broadcast_to(ref[pl.ds(r0 + r, 1), :], (S, dk))

        bref = jnp.concatenate([jnp.zeros((S, dk), F32)] + [brow(b_s, S * i - 1) for i in range(1, nsub)], axis=0)
        qs = (q * jnp.exp2(b - bref)).astype(BF16)

        blocks = [jnp.zeros((S, C), F32)]
        for i in range(1, nsub):
            n = S * i
            kt = (k[:n] * jnp.exp2(b_s[pl.ds(r0 + n - 1, 1), :] - b[:n])).astype(BF16)
            kt = jnp.concatenate([kt, jnp.zeros((C - n, dk), BF16)], axis=0)
            blocks.append(_dot_nt(qs[n:n + S], kt))
        a_off = jnp.concatenate(blocks, axis=0)

        q8 = [q[8 * g:8 * g + 8] for g in range(C // 8)]
        b8 = [b[8 * g:8 * g + 8] for g in range(C // 8)]
        zero8 = jnp.zeros((8, dk), F32)
        pieces = []
        for jj in range(S):
            parts = []
            for g in range(C // 8):
                blk, half = divmod(g, S // 8)
                lo = 8 * half
                if lo + 8 <= jj:
                    parts.append(zero8)
                    continue
                r = r0 + S * blk + jj
                e = b8[g] - jnp.broadcast_to(b_s[pl.ds(r, 1), :], (8, dk))
                if lo < jj:
                    e = jnp.minimum(e, 0.0)
                parts.append(q8[g] * jnp.exp2(e) * jnp.broadcast_to(k_s[pl.ds(r, 1), :], (8, dk)))
            pieces.append(jnp.concatenate(parts, axis=0).astype(BF16))
        a_diag = _dot(jnp.concatenate(pieces, axis=1), sel_ref[...])[:, :C]

        a = a_off + jnp.where(mask_diag, a_diag, 0.0)

        st = st_ref[...]
        qe = (q * jnp.exp2(b)).astype(BF16)
        o = _dot(a.astype(BF16), vb) + _dot_nt(qe, st.astype(BF16))

        b_last = b_s[pl.ds(r0 + C - 1, 1), :]
        kd = (k * jnp.exp2(b_last - b)).astype(BF16)
        st_ref[...] = st * jnp.exp2(b_last) + _dot_tn(vb, kd)

        gate = _silu(g_ref[0, pl.ds(r0, C), :])
        y = o * lax.rsqrt(jnp.mean(o * o, axis=-1, keepdims=True) + NORM_EPS) * gn * gate
        o_ref[pl.ds(r0, C), :] = y.astype(o_ref.dtype)
        return carry

    lax.fori_loop(0, n_chunks, chunk, 0, unroll=8)


def _hgrn_selector(dk):
    jj = jnp.arange(HGRN_SUB * dk) // dk
    c = jnp.arange(LANES)
    return ((c[None, :] % HGRN_SUB == jj[:, None]) & (c[None, :] < HGRN_CHUNK)).astype(BF16)


def hgrn_core(proj, lb, gn_w, batch, seq_len):
    nblk, t, dk = proj.shape
    heads = nblk // 4
    n_chunks = seq_len // HGRN_CHUNK

    def spec(section):
        return pl.BlockSpec((1, seq_len, dk), lambda b, h: (section * heads + h, b, 0))

    return pl.pallas_call(
        functools.partial(_hgrn_kernel, n_chunks=n_chunks),
        grid=(batch, heads),
        in_specs=[
            spec(0), spec(1), spec(2), spec(3),
            pl.BlockSpec((1, 1, dk), lambda b, h: (h, 0, 0)),
            pl.BlockSpec((1, dk), lambda b, h: (0, 0)),
            pl.BlockSpec((HGRN_SUB * dk, LANES), lambda b, h: (0, 0)),
        ],
        out_specs=pl.BlockSpec((seq_len, dk), lambda b, h: (b, h)),
        out_shape=jax.ShapeDtypeStruct((t, heads * dk), BF16),
        scratch_shapes=[
            pltpu.VMEM((dk, dk), F32),
            pltpu.VMEM((seq_len, dk), F32),
            pltpu.VMEM((seq_len, dk), F32),
            pltpu.VMEM((seq_len, dk), F32),
        ],
        compiler_params=_cparams(("parallel", "parallel")),
        name="hgrn_core",
    )(proj, proj, proj, proj, lb.reshape(heads, 1, dk), gn_w.reshape(1, dk), _hgrn_selector(dk))


LOG2E = 1.4426950408889634


def _conv_rows(x, halo, w, bias):
    ext = jnp.concatenate([halo, x], axis=0)
    acc = w[0:1] * ext
    for tap in range(1, M_CONV):
        acc = w[tap:tap + 1] * ext + pltpu.roll(acc, 1, axis=0)
    return acc[8:] + bias


def _ssd_kernel(z_ref, x_ref, bm_ref, cm_ref, xh_ref, bh_ref, ch_ref, dt_ref,
                cwx_ref, cwb_ref, cwc_ref, cbx_ref, cbb_ref, cbc_ref,
                dtb_ref, alog_ref, dskip_ref, nw_ref, ex_ref, o_ref,
                xs_ref, xdt_ref, xw_ref, ea_ref, bs_ref, cs_ref, y_ref, hst_ref, acst_ref, acsr_ref,
                cbl_ref, cbh_ref, *, hpg):
    L = M_CHUNK
    n = M_D_STATE
    half = LANES // 2
    gw = hpg * M_HEADDIM
    npairs = hpg // 2
    first = pl.program_id(1) == 0

    @pl.when(first)
    def _():
        hst_ref[...] = jnp.zeros_like(hst_ref)

    def lane_of(rows):
        return lax.broadcasted_iota(jnp.int32, (rows, LANES), 1)

    def row_of(rows):
        return lax.broadcasted_iota(jnp.int32, (rows, LANES), 0)

    left = lane_of(L) < half
    left_h = lane_of(half) < half
    left1 = lane_of(1) < half
    mask_lo = row_of(L) >= lane_of(L) % half
    mask_hi = row_of(half) >= lane_of(half) % half

    dt = _softplus(dt_ref[...] + dtb_ref[...])
    a_neg = -jnp.exp(alog_ref[...])
    acs = _cumsum_rows(dt * a_neg, _tri(L, BF16)) * LOG2E
    acst = acs.T
    acst_ref[...] = acst
    acsr_ref[...] = pltpu.roll(acst, half, axis=1)

    def split_lanes(v):
        hi = v.astype(BF16).astype(F32)
        r1 = v - hi
        mid = r1.astype(BF16).astype(F32)
        lo = (r1 - mid).astype(BF16)
        first_cols = jnp.where(left, hi, pltpu.roll(mid, half, axis=1)).astype(BF16)
        return jnp.concatenate([first_cols, lo], axis=1)

    scal = jnp.concatenate([split_lanes(acs), split_lanes(dt)], axis=0)

    slab = gw

    def prep(s, carry):
        c0 = pl.multiple_of(s * slab, slab)
        halo = jnp.where(first, 0.0, xh_ref[:, pl.ds(c0, slab)])
        xs = _silu(_conv_rows(x_ref[:, pl.ds(c0, slab)], halo, cwx_ref[:, pl.ds(c0, slab)], cbx_ref[:, pl.ds(c0, slab)]))
        xs_ref[:, pl.ds(c0, slab)] = xs
        e = _dot(scal, ex_ref[:, pl.ds(c0, slab)])
        ea = e[:L]
        ea_ref[:, pl.ds(c0, slab)] = ea
        xdt = xs * e[L:]
        xdt_ref[:, pl.ds(c0, slab)] = xdt.astype(BF16)
        xw_ref[:, pl.ds(c0, slab)] = (xdt * jnp.exp2(ea[L - 1:L] - ea)).astype(BF16)
        return carry

    lax.fori_loop(0, x_ref.shape[-1] // slab, prep, 0)

    def prep_bc(src_ref, halo_ref, w_ref, b_ref, dst_ref):
        def body(s, carry):
            c0 = pl.multiple_of(s * slab, slab)
            halo = jnp.where(first, 0.0, halo_ref[:, pl.ds(c0, slab)])
            dst_ref[:, pl.ds(c0, slab)] = _silu(_conv_rows(
                src_ref[:, pl.ds(c0, slab)], halo, w_ref[:, pl.ds(c0, slab)], b_ref[:, pl.ds(c0, slab)])
            ).astype(dst_ref.dtype)
            return carry
        lax.fori_loop(0, src_ref.shape[-1] // slab, body, 0)

    prep_bc(bm_ref, bh_ref, cwb_ref, cbb_ref, bs_ref)
    prep_bc(cm_ref, ch_ref, cwc_ref, cbc_ref, cs_ref)

    def cb_prep(g, carry):
        g0 = pl.multiple_of(g * n, n)
        cb = _dot_nt(cs_ref[:, pl.ds(g0, n)], bs_ref[:, pl.ds(g0, n)].astype(BF16))
        cbr = pltpu.roll(cb, half, axis=1)
        cbl_ref[g] = jnp.where(left, cb, cbr)
        cbh_ref[g] = jnp.where(left, cbr, cb)[half:]
        return carry

    lax.fori_loop(0, M_GROUPS, cb_prep, 0, unroll=True)

    zero = jnp.zeros((half, LANES), BF16)

    def intra(g, carry):
        c0 = pl.multiple_of(g * gw, gw)
        cb_lo = cbl_ref[g]
        cb_hi = cbh_ref[g]
        for p in range(npairs):
            x0 = pl.multiple_of(c0 + p * LANES, LANES)
            h0 = g * hpg + 2 * p
            a_col = ea_ref[:, pl.ds(x0, LANES)]
            r0 = acst_ref[pl.ds(h0, 1), :]
            r1 = acst_ref[pl.ds(h0 + 1, 1), :]
            q0 = acsr_ref[pl.ds(h0, 1), :]
            q1 = acsr_ref[pl.ds(h0 + 1, 1), :]
            a_lo = jnp.where(left1, r0, q1)
            a_hi = jnp.where(left1, q0, r1)
            p_lo = (jnp.where(mask_lo, jnp.exp2(a_col - a_lo), 0.0) * cb_lo).astype(BF16)
            p_hi = (jnp.where(mask_hi, jnp.exp2(a_col[half:] - a_hi), 0.0) * cb_hi).astype(BF16)
            xd = xdt_ref[:, pl.ds(x0, LANES)]
            x_lo, x_hi = xd[:half], xd[half:]
            bd_lo = jnp.concatenate([jnp.where(left_h, x_lo, zero), jnp.where(left_h, zero, x_lo)], axis=0)
            bd_hi = jnp.concatenate([jnp.where(left_h, x_hi, zero), jnp.where(left_h, zero, x_hi)], axis=0)
            y = _dot(p_lo, bd_lo)
            y_bot = y[half:] + _dot(p_hi, bd_hi)
            y_ref[:, pl.ds(x0, LANES)] = jnp.concatenate([y[:half], y_bot], axis=0)
        return carry

    lax.fori_loop(0, M_GROUPS, intra, 0, unroll=2)

    def inter(g, carry):
        g0 = pl.multiple_of(g * n, n)
        c0 = pl.multiple_of(g * gw, gw)
        hg = hst_ref[:, pl.ds(c0, gw)]
        ea = ea_ref[:, pl.ds(c0, gw)]
        y = y_ref[:, pl.ds(c0, gw)] + _dot(cs_ref[:, pl.ds(g0, n)], hg.astype(BF16)) * jnp.exp2(ea)
        bgt = bs_ref[:, pl.ds(g0, n)].T.astype(BF16)
        hst_ref[:, pl.ds(c0, gw)] = hg * jnp.exp2(ea[L - 1:L]) + _dot(bgt, xw_ref[:, pl.ds(c0, gw)])
        y = y + xs_ref[:, pl.ds(c0, gw)] * dskip_ref[:, pl.ds(c0, gw)]
        y = y * _silu(z_ref[:, pl.ds(c0, gw)])
        y = y * lax.rsqrt(jnp.mean(y * y, axis=-1, keepdims=True) + NORM_EPS)
        o_ref[:, pl.ds(c0, gw)] = (y * nw_ref[:, pl.ds(c0, gw)]).astype(o_ref.dtype)
        return carry

    lax.fori_loop(0, M_GROUPS, inter, 0, unroll=2)


def _head_expander(heads, d_inner):
    half = LANES // 2
    r = jnp.arange(2 * LANES)
    c = jnp.arange(d_inner)
    head_of_row = jnp.where(r < 3 * half, r % half, -1)
    return ((head_of_row[:, None] == (c // M_HEADDIM)[None, :]) & (head_of_row[:, None] < heads)).astype(BF16)


def ssd_core(zx, dt, conv_w, conv_b, dt_bias, a_log, d_skip, norm_w, batch, seq_len):
    t = zx.shape[0]
    gn = M_GROUPS * M_D_STATE
    d_inner = (zx.shape[1] - 2 * gn) // 2
    heads = d_inner // M_HEADDIM
    hpg = heads // M_GROUPS
    L = M_CHUNK
    nc = seq_len // L
    gw = hpg * M_HEADDIM
    assert d_inner % gn == 0 and heads <= LANES // 2 and hpg % 2 == 0 and L == LANES and 2 * M_HEADDIM == LANES
    assert gn % gw == 0
    r = d_inner // gn
    l8 = L // 8

    def rows(b, c):
        return b * nc + c

    def halo_rows(b, c):
        return jnp.maximum((b * nc + c) * l8 - 1, 0)

    pad = LANES - heads
    dtb = jnp.pad(dt_bias.astype(F32), (0, pad)).reshape(1, LANES)
    alog = jnp.pad(a_log.astype(F32), (0, pad)).reshape(1, LANES)
    dskip = jnp.repeat(d_skip.astype(F32), M_HEADDIM).reshape(1, d_inner)
    cb2 = conv_b.reshape(1, -1)

    in_specs = [
        pl.BlockSpec((L, d_inner), lambda b, c: (rows(b, c), 0)),
        pl.BlockSpec((L, d_inner), lambda b, c: (rows(b, c), 1)),
        pl.BlockSpec((L, gn), lambda b, c: (rows(b, c), 2 * r)),
        pl.BlockSpec((L, gn), lambda b, c: (rows(b, c), 2 * r + 1)),
        pl.BlockSpec((8, d_inner), lambda b, c: (halo_rows(b, c), 1)),
        pl.BlockSpec((8, gn), lambda b, c: (halo_rows(b, c), 2 * r)),
        pl.BlockSpec((8, gn), lambda b, c: (halo_rows(b, c), 2 * r + 1)),
        pl.BlockSpec((L, LANES), lambda b, c: (rows(b, c), 0)),
        pl.BlockSpec((M_CONV, d_inner), lambda b, c: (0, 0)),
        pl.BlockSpec((M_CONV, gn), lambda b, c: (0, r)),
        pl.BlockSpec((M_CONV, gn), lambda b, c: (0, r + 1)),
        pl.BlockSpec((1, d_inner), lambda b, c: (0, 0)),
        pl.BlockSpec((1, gn), lambda b, c: (0, r)),
        pl.BlockSpec((1, gn), lambda b, c: (0, r + 1)),
        pl.BlockSpec((1, LANES), lambda b, c: (0, 0)),
        pl.BlockSpec((1, LANES), lambda b, c: (0, 0)),
        pl.BlockSpec((1, d_inner), lambda b, c: (0, 0)),
        pl.BlockSpec((1, d_inner), lambda b, c: (0, 0)),
        pl.BlockSpec((2 * LANES, d_inner), lambda b, c: (0, 0)),
    ]
    return pl.pallas_call(
        functools.partial(_ssd_kernel, hpg=hpg),
        grid=(batch, nc),
        in_specs=in_specs,
        out_specs=pl.BlockSpec((L, d_inner), lambda b, c: (rows(b, c), 0)),
        out_shape=jax.ShapeDtypeStruct((t, d_inner), BF16),
        scratch_shapes=[
            pltpu.VMEM((L, d_inner), F32),
            pltpu.VMEM((L, d_inner), BF16),
            pltpu.VMEM((L, d_inner), BF16),
            pltpu.VMEM((L, d_inner), F32),
            pltpu.VMEM((L, gn), F32),
            pltpu.VMEM((L, gn), BF16),
            pltpu.VMEM((L, d_inner), F32),
            pltpu.VMEM((M_D_STATE, d_inner), F32),
            pltpu.VMEM((LANES, L), F32),
            pltpu.VMEM((LANES, L), F32),
            pltpu.VMEM((M_GROUPS, L, LANES), F32),
            pltpu.VMEM((M_GROUPS, L // 2, LANES), F32),
        ],
        compiler_params=_cparams(("parallel", "arbitrary")),
        name="ssd_core",
    )(zx, zx, zx, zx, zx, zx, zx, dt, conv_w, conv_w, conv_w, cb2, cb2, cb2, dtb, alog, dskip,
      norm_w.reshape(1, d_inner), _head_expander(heads, d_inner))


def kernel(x, mix_norm, ffn_norm, final_norm, hgrn_w_in, hgrn_lb_logits, hgrn_gnorm, hgrn_w_out,
           m_w_in, m_conv_w, m_conv_b, m_dt_bias, m_A_log, m_D, m_norm, m_w_out,
           f_w_up, f_conv_w, f_conv_b, f_w_down):
    batch, seq_len, d = x.shape
    depth = mix_norm.shape[0]
    t = batch * seq_len

    lb_p = jax.nn.softmax(hgrn_lb_logits.astype(F32), axis=0)
    lower_bounds = jnp.cumsum(lb_p, axis=0) - lb_p[0]

    d_inner = m_w_out.shape[1]
    m_heads = d_inner // M_HEADDIM
    zx_dim = m_w_in.shape[2] - m_heads

    h = x.reshape(t, d)
    u = rmsnorm(h, mix_norm[0], BF16)
    out = None
    for i in range(depth):
        j = i // 2
        if i % 2 == 0:
            proj = matmul_headmajor(u, hgrn_w_in[j].astype(BF16))
            y = hgrn_core(proj, lower_bounds[j], hgrn_gnorm[j], batch, seq_len)
            w_out = hgrn_w_out[j]
        else:
            w_in = m_w_in[j]
            zx = matmul(u, w_in[:, :zx_dim].astype(BF16))
            w_dt = jnp.pad(w_in[:, zx_dim:], ((0, 0), (0, LANES - m_heads))).astype(BF16)
            dt = matmul(u, w_dt, tn=LANES)
            y = ssd_core(zx, dt, m_conv_w[j], m_conv_b[j], m_dt_bias[j], m_A_log[j], m_D[j], m_norm[j],
                         batch, seq_len)
            w_out = m_w_out[j]
        h, u = matmul_res_norm(y, w_out.astype(BF16), h, ffn_norm[i], BF16)
        a = ffn_up(u, f_w_up[i].astype(BF16), f_conv_w[i], f_conv_b[i], seq_len)
        if i + 1 < depth:
            h, u = matmul_res_norm(a, f_w_down[i].astype(BF16), h, mix_norm[i + 1], BF16)
        else:
            h, out = matmul_res_norm(a, f_w_down[i].astype(BF16), h, final_norm, F32)
    return out.reshape(batch, seq_len, d)
```

```python
import functools

import jax
import jax.numpy as jnp
from jax import lax
from jax.experimental import pallas as pl
from jax.experimental.pallas import tpu as pltpu

F32 = jnp.float32
BF16 = jnp.bfloat16

NORM_EPS = 1e-5
LANES = 128
HGRN_EXPAND = 128
HGRN_CHUNK = 64
HGRN_SUB = 16
HGRN_FAST_CHUNK = 128
HGRN_FAST_SUB = 32
HGRN_FACTOR_LIMIT = 100.0
M_HEADDIM = 64
M_GROUPS = 8
M_D_STATE = 128
M_CONV = 4
M_CHUNK = 128
FFN_CONV = 3
VMEM_LIMIT = 56 * 1024 * 1024


def _cparams(sem):
    return pltpu.CompilerParams(dimension_semantics=sem, vmem_limit_bytes=VMEM_LIMIT)


def _pick(n, pref):
    t = min(pref, n)
    while n % t:
        t -= LANES if t > LANES else 8
    return t


def _dot(a, b):
    return jnp.dot(a, b, preferred_element_type=F32)


def _dot_nt(a, b):
    return lax.dot_general(a, b, (((1,), (1,)), ((), ())), preferred_element_type=F32)


def _dot_tn(a, b):
    return lax.dot_general(a, b, (((0,), (0,)), ((), ())), preferred_element_type=F32)


def _split3(x):
    hi = x.astype(BF16)
    r1 = x - hi.astype(F32)
    mid = r1.astype(BF16)
    lo = (r1 - mid.astype(F32)).astype(BF16)
    return hi, mid, lo


def _cumsum_rows(x, tri):
    hi, mid, lo = _split3(x)
    return _dot(tri, hi) + _dot(tri, mid) + _dot(tri, lo)


def _tri(n, dtype):
    r = lax.broadcasted_iota(jnp.int32, (n, n), 0)
    c = lax.broadcasted_iota(jnp.int32, (n, n), 1)
    return (r >= c).astype(dtype)


def _sigmoid(x):
    return 1.0 / (1.0 + jnp.exp(-x))


def _silu(x):
    return x * _sigmoid(x)


def _softplus(x):
    return jnp.maximum(x, 0.0) + jnp.log(1.0 + jnp.exp(-jnp.abs(x)))


def _rmsnorm_kernel(x_ref, w_ref, o_ref):
    x = x_ref[...]
    y = x * lax.rsqrt(jnp.mean(x * x, axis=-1, keepdims=True) + NORM_EPS)
    o_ref[...] = (y * w_ref[...]).astype(o_ref.dtype)


def rmsnorm(x, w, out_dtype):
    t, d = x.shape
    tm = _pick(t, 512)
    return pl.pallas_call(
        _rmsnorm_kernel,
        grid=(t // tm,),
        in_specs=[pl.BlockSpec((tm, d), lambda i: (i, 0)), pl.BlockSpec((1, d), lambda i: (0, 0))],
        out_specs=pl.BlockSpec((tm, d), lambda i: (i, 0)),
        out_shape=jax.ShapeDtypeStruct((t, d), out_dtype),
        compiler_params=_cparams(("parallel",)),
        name="rmsnorm",
    )(x, w.reshape(1, d))


def _mm_kernel(x_ref, w_ref, o_ref):
    o_ref[...] = _dot(x_ref[...], w_ref[...]).astype(o_ref.dtype)


def matmul(x, w, out_dtype=F32, tm=2048, tn=1024):
    t, k = x.shape
    n = w.shape[1]
    tm, tn = _pick(t, tm), _pick(n, tn)
    return pl.pallas_call(
        _mm_kernel,
        grid=(t // tm, n // tn),
        in_specs=[pl.BlockSpec((tm, k), lambda i, j: (i, 0)), pl.BlockSpec((k, tn), lambda i, j: (0, j))],
        out_specs=pl.BlockSpec((tm, tn), lambda i, j: (i, j)),
        out_shape=jax.ShapeDtypeStruct((t, n), out_dtype),
        compiler_params=_cparams(("parallel", "parallel")),
        name="matmul",
    )(x, w)


def _mm_heads_kernel(x_ref, w_ref, o_ref, *, nblk):
    acc = _dot(x_ref[...], w_ref[...])
    for c in range(nblk):
        o_ref[c] = acc[:, c * LANES:(c + 1) * LANES]


def matmul_headmajor(x, w, tm=2048, tn=1024):
    t, k = x.shape
    n = w.shape[1]
    tm, tn = _pick(t, tm), _pick(n, tn)
    nblk = tn // LANES
    return pl.pallas_call(
        functools.partial(_mm_heads_kernel, nblk=nblk),
        grid=(t // tm, n // tn),
        in_specs=[pl.BlockSpec((tm, k), lambda i, j: (i, 0)), pl.BlockSpec((k, tn), lambda i, j: (0, j))],
        out_specs=pl.BlockSpec((nblk, tm, LANES), lambda i, j: (j, i, 0)),
        out_shape=jax.ShapeDtypeStruct((n // LANES, t, LANES), F32),
        compiler_params=_cparams(("parallel", "parallel")),
        name="matmul_headmajor",
    )(x, w)


def _mm_res_norm_kernel(x_ref, w_ref, h_ref, nw_ref, h_out_ref, u_out_ref):
    h = h_ref[...] + _dot(x_ref[...], w_ref[...])
    h_out_ref[...] = h
    y = h * lax.rsqrt(jnp.mean(h * h, axis=-1, keepdims=True) + NORM_EPS)
    u_out_ref[...] = (y * nw_ref[...]).astype(u_out_ref.dtype)


def matmul_res_norm(x, w, h, norm_w, u_dtype):
    t, kdim = x.shape
    d = w.shape[1]
    row_bytes = 2 * 2 * kdim + (4 * 4 + 2 * jnp.dtype(u_dtype).itemsize + 4) * d
    budget = VMEM_LIMIT - 2 * kdim * d - (6 << 20)
    tm = 1024
    while tm * row_bytes > budget or t % tm:
        tm //= 2
    return pl.pallas_call(
        _mm_res_norm_kernel,
        grid=(t // tm,),
        in_specs=[
            pl.BlockSpec((tm, kdim), lambda i: (i, 0)),
            pl.BlockSpec((kdim, d), lambda i: (0, 0), pipeline_mode=pl.Buffered(1)),
            pl.BlockSpec((tm, d), lambda i: (i, 0)),
            pl.BlockSpec((1, d), lambda i: (0, 0)),
        ],
        out_specs=[pl.BlockSpec((tm, d), lambda i: (i, 0)), pl.BlockSpec((tm, d), lambda i: (i, 0))],
        out_shape=[jax.ShapeDtypeStruct((t, d), F32), jax.ShapeDtypeStruct((t, d), u_dtype)],
        compiler_params=_cparams(("parallel",)),
        name="matmul_res_norm",
    )(x, w, h, norm_w.reshape(1, d))


def _shift_rows(x, halo, s):
    rolled = pltpu.roll(x, s, axis=0)
    row = lax.broadcasted_iota(jnp.int32, halo.shape, 0)
    top = jnp.where(row < s, pltpu.roll(halo, s, axis=0), rolled[:8])
    return jnp.concatenate([top, rolled[8:]], axis=0)


def _ffn_up_kernel(u_ref, uh_ref, wg_ref, wu_ref, cwg_ref, cwu_ref, cbg_ref, cbu_ref, o_ref, *, tiles_per_seq):
    i = pl.program_id(0)
    first = (i % tiles_per_seq) == 0
    u = u_ref[...]
    uh = jnp.where(first, jnp.zeros_like(uh_ref[...]), uh_ref[...])

    def conv(w_ref, cw_ref, cb_ref):
        x = _dot(u, w_ref[...])
        xh = _dot(uh, w_ref[...])[8:]
        cw = cw_ref[...]
        return (cb_ref[...] + cw[2:3] * x + cw[1:2] * _shift_rows(x, xh, 1) + cw[0:1] * _shift_rows(x, xh, 2))

    g = conv(wg_ref, cwg_ref, cbg_ref)
    up = conv(wu_ref, cwu_ref, cbu_ref)
    o_ref[...] = (_silu(g) * up).astype(o_ref.dtype)


def ffn_up(u, w_up, conv_w, conv_b, seq_len, tm=1024, tn=512):
    t, d = u.shape
    dff = w_up.shape[1] // 2
    tm = _pick(seq_len, tm)
    tn = _pick(dff, tn)
    nj = dff // tn
    tm16 = tm // 16
    return pl.pallas_call(
        functools.partial(_ffn_up_kernel, tiles_per_seq=seq_len // tm),
        grid=(t // tm, nj),
        in_specs=[
            pl.BlockSpec((tm, d), lambda i, j: (i, 0)),
            pl.BlockSpec((16, d), lambda i, j: (jnp.maximum(i * tm16 - 1, 0), 0)),
            pl.BlockSpec((d, tn), lambda i, j: (0, j)),
            pl.BlockSpec((d, tn), lambda i, j: (0, j + nj)),
            pl.BlockSpec((FFN_CONV, tn), lambda i, j: (0, j)),
            pl.BlockSpec((FFN_CONV, tn), lambda i, j: (0, j + nj)),
            pl.BlockSpec((1, tn), lambda i, j: (0, j)),
            pl.BlockSpec((1, tn), lambda i, j: (0, j + nj)),
        ],
        out_specs=pl.BlockSpec((tm, tn), lambda i, j: (i, j)),
        out_shape=jax.ShapeDtypeStruct((t, dff), BF16),
        compiler_params=_cparams(("parallel", "parallel")),
        name="ffn_up",
    )(u, u, w_up, w_up, conv_w, conv_w, conv_b.reshape(1, -1), conv_b.reshape(1, -1))


def _hgrn_kernel(q_ref, f_ref, v_ref, g_ref, lb_ref, gn_ref, sel_ref, o_ref, st_ref, q_s, k_s, b_s, *, n_chunks):
    C, S = HGRN_CHUNK, HGRN_SUB
    nsub = C // S
    dk = q_ref.shape[-1]
    lb = lb_ref[0]
    one_m_lb = 1.0 - lb
    gn = gn_ref[...]
    tri = _tri(C, BF16)
    row = lax.broadcasted_iota(jnp.int32, (C, C), 0)
    col = lax.broadcasted_iota(jnp.int32, (C, C), 1)
    mask_diag = ((col // S) == (row // S)) & (col <= row)

    def prepare(ci, worst):
        r0 = pl.multiple_of(ci * C, C)
        sg = _sigmoid(f_ref[0, pl.ds(r0, C), :])
        q_s[pl.ds(r0, C), :] = _silu(q_ref[0, pl.ds(r0, C), :])
        k_s[pl.ds(r0, C), :] = one_m_lb * (1.0 - sg)
        b = _cumsum_rows(jnp.log2(lb + one_m_lb * sg), tri)
        b_s[pl.ds(r0, C), :] = b
        mid = b[C // 2 - 1:C // 2]
        return jnp.minimum(worst, jnp.minimum(mid, b[C - 1:C] - mid))

    worst = lax.fori_loop(0, n_chunks, prepare, jnp.zeros((1, dk), F32), unroll=8)
    assert HGRN_FAST_SUB == C // 2 and HGRN_FAST_CHUNK == 2 * C
    bounded = jnp.min(worst) > -HGRN_FACTOR_LIMIT

    st_ref[...] = jnp.zeros_like(st_ref)

    def scaled(r0, x, b, refs, sub, sign):
        ref = jnp.concatenate([jnp.zeros((sub, dk), F32) if r is None else jnp.broadcast_to(r, (sub, dk))
                               for r in refs], axis=0)
        return (x * jnp.exp2(sign * (b - ref))).astype(BF16)

    def finish(r0, rows, q, k, b, vb, a):
        st = st_ref[...]
        qe = (q * jnp.exp2(b)).astype(BF16)
        o = _dot(a.astype(BF16), vb) + _dot_nt(qe, st.astype(BF16))
        b_last = b[rows - 1:rows]
        kd = (k * jnp.exp2(b_last - b)).astype(BF16)
        st_ref[...] = st * jnp.exp2(b_last) + _dot_tn(vb, kd)
        gate = _silu(g_ref[0, pl.ds(r0, rows), :])
        y = o * lax.rsqrt(jnp.mean(o * o, axis=-1, keepdims=True) + NORM_EPS) * gn * gate
        o_ref[pl.ds(r0, rows), :] = y.astype(o_ref.dtype)

    CF, SF = HGRN_FAST_CHUNK, HGRN_FAST_SUB
    rowf = lax.broadcasted_iota(jnp.int32, (CF, CF), 0)
    colf = lax.broadcasted_iota(jnp.int32, (CF, CF), 1)

    def chunk_factorised(ci, carry):
        r0 = pl.multiple_of(ci * CF, CF)
        q = q_s[pl.ds(r0, CF), :]
        k = k_s[pl.ds(r0, CF), :]
        vb = v_ref[0, pl.ds(r0, CF), :].astype(BF16)
        b_top = b_s[pl.ds(r0, C), :]
        b = jnp.concatenate([b_top, b_s[pl.ds(r0 + C, C), :] + b_top[C - 1:C]], axis=0)
        refs = [None] + [b[SF * i - 1:SF * i] for i in range(1, CF // SF)]
        qs = scaled(r0, q, b, refs, SF, 1.0)
        blocks = []
        for i, ref in enumerate(refs):
            n = SF * (i + 1)
            kt = (k[:n] * jnp.exp2((0.0 if ref is None else ref) - b[:n])).astype(BF16)
            if n < CF:
                kt = jnp.concatenate([kt, jnp.zeros((CF - n, dk), BF16)], axis=0)
            blocks.append(_dot_nt(qs[SF * i:SF * (i + 1)], kt))
        a = jnp.where(colf <= rowf, jnp.concatenate(blocks, axis=0), 0.0)
        finish(r0, CF, q, k, b, vb, a)
        return carry

    def chunk_pairwise(ci, carry):
        r0 = pl.multiple_of(ci * C, C)
        q = q_s[pl.ds(r0, C), :]
        k = k_s[pl.ds(r0, C), :]
        b = b_s[pl.ds(r0, C), :]
        vb = v_ref[0, pl.ds(r0, C), :].astype(BF16)
        refs = [None] + [b[S * i - 1:S * i] for i in range(1, nsub)]
        qs = scaled(r0, q, b, refs, S, 1.0)
        blocks = [jnp.zeros((S, C), F32)]
        for i in range(1, nsub):
            n = S * i
            kt = (k[:n] * jnp.exp2(refs[i] - b[:n])).astype(BF16)
            kt = jnp.concatenate([kt, jnp.zeros((C - n, dk), BF16)], axis=0)
            blocks.append(_dot_nt(qs[n:n + S], kt))
        a_off = jnp.concatenate(blocks, axis=0)

        q8 = [q[8 * g:8 * g + 8] for g in range(C // 8)]
        b8 = [b[8 * g:8 * g + 8] for g in range(C // 8)]
        zero8 = jnp.zeros((8, dk), F32)
        pieces = []
        for jj in range(S):
            parts = []
            for g in range(C // 8):
                blk, half = divmod(g, S // 8)
                lo = 8 * half
                if lo + 8 <= jj:
                    parts.append(zero8)
                    continue
                r = r0 + S * blk + jj
                e = b8[g] - jnp.broadcast_to(b_s[pl.ds(r, 1), :], (8, dk))
                if lo < jj:
                    e = jnp.minimum(e, 0.0)
                parts.append(q8[g] * jnp.exp2(e) * jnp.broadcast_to(k_s[pl.ds(r, 1), :], (8, dk)))
            pieces.append(jnp.concatenate(parts, axis=0).astype(BF16))
        a_diag = _dot(jnp.concatenate(pieces, axis=1), sel_ref[...])[:, :C]
        finish(r0, C, q, k, b, vb, a_off + jnp.where(mask_diag, a_diag, 0.0))
        return carry

    @pl.when(bounded)
    def _():
        lax.fori_loop(0, n_chunks * C // CF, chunk_factorised, 0, unroll=8)

    @pl.when(jnp.logical_not(bounded))
    def _():
        lax.fori_loop(0, n_chunks, chunk_pairwise, 0, unroll=8)


def _hgrn_selector(dk):
    jj = jnp.arange(HGRN_SUB * dk) // dk
    c = jnp.arange(LANES)
    return ((c[None, :] % HGRN_SUB == jj[:, None]) & (c[None, :] < HGRN_CHUNK)).astype(BF16)


def hgrn_core(proj, lb, gn_w, batch, seq_len):
    nblk, t, dk = proj.shape
    heads = nblk // 4
    n_chunks = seq_len // HGRN_CHUNK

    def spec(section):
        return pl.BlockSpec((1, seq_len, dk), lambda b, h: (section * heads + h, b, 0))

    return pl.pallas_call(
        functools.partial(_hgrn_kernel, n_chunks=n_chunks),
        grid=(batch, heads),
        in_specs=[
            spec(0), spec(1), spec(2), spec(3),
            pl.BlockSpec((1, 1, dk), lambda b, h: (h, 0, 0)),
            pl.BlockSpec((1, dk), lambda b, h: (0, 0)),
            pl.BlockSpec((HGRN_SUB * dk, LANES), lambda b, h: (0, 0)),
        ],
        out_specs=pl.BlockSpec((seq_len, dk), lambda b, h: (b, h)),
        out_shape=jax.ShapeDtypeStruct((t, heads * dk), BF16),
        scratch_shapes=[
            pltpu.VMEM((dk, dk), F32),
            pltpu.VMEM((seq_len, dk), F32),
            pltpu.VMEM((seq_len, dk), F32),
            pltpu.VMEM((seq_len, dk), F32),
        ],
        compiler_params=_cparams(("parallel", "parallel")),
        name="hgrn_core",
    )(proj, proj, proj, proj, lb.reshape(heads, 1, dk), gn_w.reshape(1, dk), _hgrn_selector(dk))


LOG2E = 1.4426950408889634


CONV_PAD = 16


def _conv_rows(x, halo, w, bias, shift):
    ext = jnp.concatenate([jnp.zeros_like(halo), halo, x], axis=0)
    taps = jnp.concatenate([(w[t:t + 1] * ext).astype(BF16) for t in range(M_CONV)], axis=0)
    return _dot(shift, taps) + bias


def _conv_shift_matrix(rows):
    width = CONV_PAD + rows
    r = jnp.arange(rows)[:, None]
    c = jnp.arange(M_CONV * width)[None, :]
    tap, src = c // width, c % width
    return (src == CONV_PAD + r - (M_CONV - 1 - tap)).astype(BF16)


def _ssd_kernel(z_ref, x_ref, bm_ref, cm_ref, xh_ref, bh_ref, ch_ref, dt_ref,
                cwx_ref, cwb_ref, cwc_ref, cbx_ref, cbb_ref, cbc_ref,
                dtb_ref, alog_ref, dskip_ref, nw_ref, ex_ref, sh_ref, o_ref,
                xs_ref, xdt_ref, xw_ref, ea_ref, bs_ref, cs_ref, y_ref, hst_ref, acst_ref, acsr_ref,
                cbl_ref, cbh_ref, *, hpg):
    L = M_CHUNK
    n = M_D_STATE
    half = LANES // 2
    gw = hpg * M_HEADDIM
    npairs = hpg // 2
    first = pl.program_id(1) == 0

    @pl.when(first)
    def _():
        hst_ref[...] = jnp.zeros_like(hst_ref)

    def lane_of(rows):
        return lax.broadcasted_iota(jnp.int32, (rows, LANES), 1)

    def row_of(rows):
        return lax.broadcasted_iota(jnp.int32, (rows, LANES), 0)

    left = lane_of(L) < half
    left_h = lane_of(half) < half
    left1 = lane_of(1) < half
    mask_lo = row_of(L) >= lane_of(L) % half
    mask_hi = row_of(half) >= lane_of(half) % half

    dt = _softplus(dt_ref[...] + dtb_ref[...])
    a_neg = -jnp.exp(alog_ref[...])
    acs = _cumsum_rows(dt * a_neg, _tri(L, BF16)) * LOG2E
    acst = acs.T
    acst_ref[...] = acst
    acsr_ref[...] = pltpu.roll(acst, half, axis=1)

    def split_lanes(v):
        hi = v.astype(BF16).astype(F32)
        r1 = v - hi
        mid = r1.astype(BF16).astype(F32)
        lo = (r1 - mid).astype(BF16)
        first_cols = jnp.where(left, hi, pltpu.roll(mid, half, axis=1)).astype(BF16)
        return jnp.concatenate([first_cols, lo], axis=1)

    scal = jnp.concatenate([split_lanes(acs), split_lanes(dt)], axis=0)

    slab = gw

    def prep(s, carry):
        c0 = pl.multiple_of(s * slab, slab)
        halo = jnp.where(first, 0.0, xh_ref[:, pl.ds(c0, slab)])
        xs = _silu(_conv_rows(x_ref[:, pl.ds(c0, slab)], halo, cwx_ref[:, pl.ds(c0, slab)],
                              cbx_ref[:, pl.ds(c0, slab)], sh_ref[...]))
        xs_ref[:, pl.ds(c0, slab)] = xs
        e = _dot(scal, ex_ref[:, pl.ds(c0, slab)])
        ea = e[:L]
        ea_ref[:, pl.ds(c0, slab)] = ea
        xdt = xs * e[L:]
        xdt_ref[:, pl.ds(c0, slab)] = xdt.astype(BF16)
        xw_ref[:, pl.ds(c0, slab)] = (xdt * jnp.exp2(ea[L - 1:L] - ea)).astype(BF16)
        return carry

    lax.fori_loop(0, x_ref.shape[-1] // slab, prep, 0, unroll=2)

    def prep_bc(src_ref, halo_ref, w_ref, b_ref, dst_ref):
        def body(s, carry):
            c0 = pl.multiple_of(s * slab, slab)
            halo = jnp.where(first, 0.0, halo_ref[:, pl.ds(c0, slab)])
            dst_ref[:, pl.ds(c0, slab)] = _silu(_conv_rows(
                src_ref[:, pl.ds(c0, slab)], halo, w_ref[:, pl.ds(c0, slab)], b_ref[:, pl.ds(c0, slab)],
                sh_ref[...])).astype(dst_ref.dtype)
            return carry
        lax.fori_loop(0, src_ref.shape[-1] // slab, body, 0, unroll=2)

    prep_bc(bm_ref, bh_ref, cwb_ref, cbb_ref, bs_ref)
    prep_bc(cm_ref, ch_ref, cwc_ref, cbc_ref, cs_ref)

    def cb_prep(g, carry):
        g0 = pl.multiple_of(g * n, n)
        cb = _dot_nt(cs_ref[:, pl.ds(g0, n)], bs_ref[:, pl.ds(g0, n)].astype(BF16))
        cbr = pltpu.roll(cb, half, axis=1)
        cbl_ref[g] = jnp.where(left, cb, cbr)
        cbh_ref[g] = jnp.where(left, cbr, cb)[half:]
        return carry

    lax.fori_loop(0, M_GROUPS, cb_prep, 0, unroll=True)

    zero = jnp.zeros((half, LANES), BF16)

    def intra(g, carry):
        c0 = pl.multiple_of(g * gw, gw)
        cb_lo = cbl_ref[g]
        cb_hi = cbh_ref[g]
        for p in range(npairs):
            x0 = pl.multiple_of(c0 + p * LANES, LANES)
            h0 = g * hpg + 2 * p
            a_col = ea_ref[:, pl.ds(x0, LANES)]
            r0 = acst_ref[pl.ds(h0, 1), :]
            r1 = acst_ref[pl.ds(h0 + 1, 1), :]
            q0 = acsr_ref[pl.ds(h0, 1), :]
            q1 = acsr_ref[pl.ds(h0 + 1, 1), :]
            a_lo = jnp.where(left1, r0, q1)
            a_hi = jnp.where(left1, q0, r1)
            p_lo = (jnp.where(mask_lo, jnp.exp2(a_col - a_lo), 0.0) * cb_lo).astype(BF16)
            p_hi = (jnp.where(mask_hi, jnp.exp2(a_col[half:] - a_hi), 0.0) * cb_hi).astype(BF16)
            xd = xdt_ref[:, pl.ds(x0, LANES)]
            x_lo, x_hi = xd[:half], xd[half:]
            bd_lo = jnp.concatenate([jnp.where(left_h, x_lo, zero), jnp.where(left_h, zero, x_lo)], axis=0)
            bd_hi = jnp.concatenate([jnp.where(left_h, x_hi, zero), jnp.where(left_h, zero, x_hi)], axis=0)
            y = _dot(p_lo, bd_lo)
            y_bot = y[half:] + _dot(p_hi, bd_hi)
            y_ref[:, pl.ds(x0, LANES)] = jnp.concatenate([y[:half], y_bot], axis=0)
        return carry

    lax.fori_loop(0, M_GROUPS, intra, 0, unroll=2)

    def inter(g, carry):
        g0 = pl.multiple_of(g * n, n)
        c0 = pl.multiple_of(g * gw, gw)
        hg = hst_ref[:, pl.ds(c0, gw)]
        ea = ea_ref[:, pl.ds(c0, gw)]
        y = y_ref[:, pl.ds(c0, gw)] + _dot(cs_ref[:, pl.ds(g0, n)], hg.astype(BF16)) * jnp.exp2(ea)
        bgt = bs_ref[:, pl.ds(g0, n)].T.astype(BF16)
        hst_ref[:, pl.ds(c0, gw)] = hg * jnp.exp2(ea[L - 1:L]) + _dot(bgt, xw_ref[:, pl.ds(c0, gw)])
        y = y + xs_ref[:, pl.ds(c0, gw)] * dskip_ref[:, pl.ds(c0, gw)]
        y = y * _silu(z_ref[:, pl.ds(c0, gw)])
        y = y * lax.rsqrt(jnp.mean(y * y, axis=-1, keepdims=True) + NORM_EPS)
        o_ref[:, pl.ds(c0, gw)] = (y * nw_ref[:, pl.ds(c0, gw)]).astype(o_ref.dtype)
        return carry

    lax.fori_loop(0, M_GROUPS, inter, 0, unroll=2)


def _head_expander(heads, d_inner):
    half = LANES // 2
    r = jnp.arange(2 * LANES)
    c = jnp.arange(d_inner)
    head_of_row = jnp.where(r < 3 * half, r % half, -1)
    return ((head_of_row[:, None] == (c // M_HEADDIM)[None, :]) & (head_of_row[:, None] < heads)).astype(BF16)


def ssd_core(zx, dt, conv_w, conv_b, dt_bias, a_log, d_skip, norm_w, batch, seq_len):
    t = zx.shape[0]
    gn = M_GROUPS * M_D_STATE
    d_inner = (zx.shape[1] - 2 * gn) // 2
    heads = d_inner // M_HEADDIM
    hpg = heads // M_GROUPS
    L = M_CHUNK
    nc = seq_len // L
    gw = hpg * M_HEADDIM
    assert d_inner % gn == 0 and heads <= LANES // 2 and hpg % 2 == 0 and L == LANES and 2 * M_HEADDIM == LANES
    assert gn % gw == 0
    r = d_inner // gn
    l8 = L // 8

    def rows(b, c):
        return b * nc + c

    def halo_rows(b, c):
        return jnp.maximum((b * nc + c) * l8 - 1, 0)

    pad = LANES - heads
    dtb = jnp.pad(dt_bias.astype(F32), (0, pad)).reshape(1, LANES)
    alog = jnp.pad(a_log.astype(F32), (0, pad)).reshape(1, LANES)
    dskip = jnp.repeat(d_skip.astype(F32), M_HEADDIM).reshape(1, d_inner)
    cb2 = conv_b.reshape(1, -1)

    in_specs = [
        pl.BlockSpec((L, d_inner), lambda b, c: (rows(b, c), 0)),
        pl.BlockSpec((L, d_inner), lambda b, c: (rows(b, c), 1)),
        pl.BlockSpec((L, gn), lambda b, c: (rows(b, c), 2 * r)),
        pl.BlockSpec((L, gn), lambda b, c: (rows(b, c), 2 * r + 1)),
        pl.BlockSpec((8, d_inner), lambda b, c: (halo_rows(b, c), 1)),
        pl.BlockSpec((8, gn), lambda b, c: (halo_rows(b, c), 2 * r)),
        pl.BlockSpec((8, gn), lambda b, c: (halo_rows(b, c), 2 * r + 1)),
        pl.BlockSpec((L, LANES), lambda b, c: (rows(b, c), 0)),
        pl.BlockSpec((M_CONV, d_inner), lambda b, c: (0, 0)),
        pl.BlockSpec((M_CONV, gn), lambda b, c: (0, r)),
        pl.BlockSpec((M_CONV, gn), lambda b, c: (0, r + 1)),
        pl.BlockSpec((1, d_inner), lambda b, c: (0, 0)),
        pl.BlockSpec((1, gn), lambda b, c: (0, r)),
        pl.BlockSpec((1, gn), lambda b, c: (0, r + 1)),
        pl.BlockSpec((1, LANES), lambda b, c: (0, 0)),
        pl.BlockSpec((1, LANES), lambda b, c: (0, 0)),
        pl.BlockSpec((1, d_inner), lambda b, c: (0, 0)),
        pl.BlockSpec((1, d_inner), lambda b, c: (0, 0)),
        pl.BlockSpec((2 * LANES, d_inner), lambda b, c: (0, 0)),
        pl.BlockSpec((L, M_CONV * (CONV_PAD + L)), lambda b, c: (0, 0)),
    ]
    return pl.pallas_call(
        functools.partial(_ssd_kernel, hpg=hpg),
        grid=(batch, nc),
        in_specs=in_specs,
        out_specs=pl.BlockSpec((L, d_inner), lambda b, c: (rows(b, c), 0)),
        out_shape=jax.ShapeDtypeStruct((t, d_inner), BF16),
        scratch_shapes=[
            pltpu.VMEM((L, d_inner), F32),
            pltpu.VMEM((L, d_inner), BF16),
            pltpu.VMEM((L, d_inner), BF16),
            pltpu.VMEM((L, d_inner), F32),
            pltpu.VMEM((L, gn), F32),
            pltpu.VMEM((L, gn), BF16),
            pltpu.VMEM((L, d_inner), F32),
            pltpu.VMEM((M_D_STATE, d_inner), F32),
            pltpu.VMEM((LANES, L), F32),
            pltpu.VMEM((LANES, L), F32),
            pltpu.VMEM((M_GROUPS, L, LANES), F32),
            pltpu.VMEM((M_GROUPS, L // 2, LANES), F32),
        ],
        compiler_params=_cparams(("parallel", "arbitrary")),
        name="ssd_core",
    )(zx, zx, zx, zx, zx, zx, zx, dt, conv_w, conv_w, conv_w, cb2, cb2, cb2, dtb, alog, dskip,
      norm_w.reshape(1, d_inner), _head_expander(heads, d_inner), _conv_shift_matrix(L))


def kernel(x, mix_norm, ffn_norm, final_norm, hgrn_w_in, hgrn_lb_logits, hgrn_gnorm, hgrn_w_out,
           m_w_in, m_conv_w, m_conv_b, m_dt_bias, m_A_log, m_D, m_norm, m_w_out,
           f_w_up, f_conv_w, f_conv_b, f_w_down):
    batch, seq_len, d = x.shape
    depth = mix_norm.shape[0]
    t = batch * seq_len

    lb_p = jax.nn.softmax(hgrn_lb_logits.astype(F32), axis=0)
    lower_bounds = jnp.cumsum(lb_p, axis=0) - lb_p[0]

    d_inner = m_w_out.shape[1]
    m_heads = d_inner // M_HEADDIM
    zx_dim = m_w_in.shape[2] - m_heads

    h = x.reshape(t, d)
    u = rmsnorm(h, mix_norm[0], BF16)
    out = None
    for i in range(depth):
        j = i // 2
        if i % 2 == 0:
            proj = matmul_headmajor(u, hgrn_w_in[j].astype(BF16))
            y = hgrn_core(proj, lower_bounds[j], hgrn_gnorm[j], batch, seq_len)
            w_out = hgrn_w_out[j]
        else:
            w_in = m_w_in[j]
            zx = matmul(u, w_in[:, :zx_dim].astype(BF16))
            w_dt = jnp.pad(w_in[:, zx_dim:], ((0, 0), (0, LANES - m_heads))).astype(BF16)
            dt = matmul(u, w_dt, tn=LANES)
            y = ssd_core(zx, dt, m_conv_w[j], m_conv_b[j], m_dt_bias[j], m_A_log[j], m_D[j], m_norm[j],
                         batch, seq_len)
            w_out = m_w_out[j]
        h, u = matmul_res_norm(y, w_out.astype(BF16), h, ffn_norm[i], BF16)
        a = ffn_up(u, f_w_up[i].astype(BF16), f_conv_w[i], f_conv_b[i], seq_len)
        if i + 1 < depth:
            h, u = matmul_res_norm(a, f_w_down[i].astype(BF16), h, mix_norm[i + 1], BF16)
        else:
            h, out = matmul_res_norm(a, f_w_down[i].astype(BF16), h, final_norm, F32)
    return out.reshape(batch, seq_len, d)
```

```python
import functools

import jax
import jax.numpy as jnp
from jax import lax
from jax.experimental import pallas as pl
from jax.experimental.pallas import tpu as pltpu

F32 = jnp.float32
BF16 = jnp.bfloat16

NORM_EPS = 1e-5
LANES = 128
HGRN_EXPAND = 128
HGRN_CHUNK = 64
HGRN_SUB = 16
HGRN_FAST_CHUNK = 128
HGRN_FAST_SUB = 32
HGRN_FACTOR_LIMIT = 100.0
M_HEADDIM = 64
M_GROUPS = 8
M_D_STATE = 128
M_CONV = 4
M_CHUNK = 128
FFN_CONV = 3
VMEM_LIMIT = 56 * 1024 * 1024


def _cparams(sem):
    return pltpu.CompilerParams(dimension_semantics=sem, vmem_limit_bytes=VMEM_LIMIT)


def _pick(n, pref):
    t = min(pref, n)
    while n % t:
        t -= LANES if t > LANES else 8
    return t


def _dot(a, b):
    return jnp.dot(a, b, preferred_element_type=F32)


def _dot_nt(a, b):
    return lax.dot_general(a, b, (((1,), (1,)), ((), ())), preferred_element_type=F32)


def _dot_tn(a, b):
    return lax.dot_general(a, b, (((0,), (0,)), ((), ())), preferred_element_type=F32)


def _split3(x):
    hi = x.astype(BF16)
    r1 = x - hi.astype(F32)
    mid = r1.astype(BF16)
    lo = (r1 - mid.astype(F32)).astype(BF16)
    return hi, mid, lo


def _cumsum_rows(x, tri):
    hi, mid, lo = _split3(x)
    return _dot(tri, hi) + _dot(tri, mid) + _dot(tri, lo)


def _tri(n, dtype):
    r = lax.broadcasted_iota(jnp.int32, (n, n), 0)
    c = lax.broadcasted_iota(jnp.int32, (n, n), 1)
    return (r >= c).astype(dtype)


def _sigmoid(x):
    return 1.0 / (1.0 + jnp.exp(-x))


def _silu(x):
    return x * _sigmoid(x)


def _softplus(x):
    return jnp.maximum(x, 0.0) + jnp.log(1.0 + jnp.exp(-jnp.abs(x)))


def _rmsnorm_kernel(x_ref, w_ref, o_ref):
    x = x_ref[...]
    y = x * lax.rsqrt(jnp.mean(x * x, axis=-1, keepdims=True) + NORM_EPS)
    o_ref[...] = (y * w_ref[...]).astype(o_ref.dtype)


def rmsnorm(x, w, out_dtype):
    t, d = x.shape
    tm = _pick(t, 512)
    return pl.pallas_call(
        _rmsnorm_kernel,
        grid=(t // tm,),
        in_specs=[pl.BlockSpec((tm, d), lambda i: (i, 0)), pl.BlockSpec((1, d), lambda i: (0, 0))],
        out_specs=pl.BlockSpec((tm, d), lambda i: (i, 0)),
        out_shape=jax.ShapeDtypeStruct((t, d), out_dtype),
        compiler_params=_cparams(("parallel",)),
        name="rmsnorm",
    )(x, w.reshape(1, d))


def _mm_kernel(x_ref, w_ref, o_ref):
    o_ref[...] = _dot(x_ref[...], w_ref[...]).astype(o_ref.dtype)


def matmul(x, w, layer, n, out_dtype=F32, tm=2048, tn=1024):
    t, k = x.shape
    tm, tn = _pick(t, tm), _pick(n, tn)
    return pl.pallas_call(
        _mm_kernel,
        grid=(t // tm, n // tn),
        in_specs=[pl.BlockSpec((tm, k), lambda i, j: (i, 0)),
                  pl.BlockSpec((None, k, tn), lambda i, j: (layer, 0, j))],
        out_specs=pl.BlockSpec((tm, tn), lambda i, j: (i, j)),
        out_shape=jax.ShapeDtypeStruct((t, n), out_dtype),
        compiler_params=_cparams(("parallel", "parallel")),
        name="matmul",
    )(x, w)


def _mm_heads_kernel(x_ref, w_ref, o_ref, *, nblk):
    acc = _dot(x_ref[...], w_ref[...])
    for c in range(nblk):
        o_ref[c] = acc[:, c * LANES:(c + 1) * LANES]


def matmul_headmajor(x, w, layer, tm=2048, tn=1024):
    t, k = x.shape
    n = w.shape[2]
    tm, tn = _pick(t, tm), _pick(n, tn)
    nblk = tn // LANES
    return pl.pallas_call(
        functools.partial(_mm_heads_kernel, nblk=nblk),
        grid=(t // tm, n // tn),
        in_specs=[pl.BlockSpec((tm, k), lambda i, j: (i, 0)),
                  pl.BlockSpec((None, k, tn), lambda i, j: (layer, 0, j))],
        out_specs=pl.BlockSpec((nblk, tm, LANES), lambda i, j: (j, i, 0)),
        out_shape=jax.ShapeDtypeStruct((n // LANES, t, LANES), F32),
        compiler_params=_cparams(("parallel", "parallel")),
        name="matmul_headmajor",
    )(x, w)


def _mm_res_norm_kernel(x_ref, w_ref, h_ref, nw_ref, h_out_ref, u_out_ref):
    h = h_ref[...] + _dot(x_ref[...], w_ref[...])
    h_out_ref[...] = h
    y = h * lax.rsqrt(jnp.mean(h * h, axis=-1, keepdims=True) + NORM_EPS)
    u_out_ref[...] = (y * nw_ref[...]).astype(u_out_ref.dtype)


def matmul_res_norm(x, w, layer, h, norm_w, u_dtype):
    t, kdim = x.shape
    d = w.shape[2]
    row_bytes = 2 * 2 * kdim + (4 * 4 + 2 * jnp.dtype(u_dtype).itemsize + 4) * d
    budget = VMEM_LIMIT - 2 * kdim * d - (6 << 20)
    tm = 1024
    while tm * row_bytes > budget or t % tm:
        tm //= 2
    return pl.pallas_call(
        _mm_res_norm_kernel,
        grid=(t // tm,),
        in_specs=[
            pl.BlockSpec((tm, kdim), lambda i: (i, 0)),
            pl.BlockSpec((None, kdim, d), lambda i: (layer, 0, 0), pipeline_mode=pl.Buffered(1)),
            pl.BlockSpec((tm, d), lambda i: (i, 0)),
            pl.BlockSpec((1, d), lambda i: (0, 0)),
        ],
        out_specs=[pl.BlockSpec((tm, d), lambda i: (i, 0)), pl.BlockSpec((tm, d), lambda i: (i, 0))],
        out_shape=[jax.ShapeDtypeStruct((t, d), F32), jax.ShapeDtypeStruct((t, d), u_dtype)],
        compiler_params=_cparams(("parallel",)),
        name="matmul_res_norm",
    )(x, w, h, norm_w.reshape(1, d))


def _shift_rows(x, halo, s):
    rolled = pltpu.roll(x, s, axis=0)
    row = lax.broadcasted_iota(jnp.int32, halo.shape, 0)
    top = jnp.where(row < s, pltpu.roll(halo, s, axis=0), rolled[:8])
    return jnp.concatenate([top, rolled[8:]], axis=0)


def _ffn_up_kernel(u_ref, uh_ref, wg_ref, wu_ref, cwg_ref, cwu_ref, cbg_ref, cbu_ref, o_ref, *, tiles_per_seq):
    i = pl.program_id(0)
    first = (i % tiles_per_seq) == 0
    u = u_ref[...]
    uh = jnp.where(first, jnp.zeros_like(uh_ref[...]), uh_ref[...])

    def conv(w_ref, cw_ref, cb_ref):
        x = _dot(u, w_ref[...])
        xh = _dot(uh, w_ref[...])[8:]
        cw = cw_ref[...]
        return (cb_ref[...] + cw[2:3] * x + cw[1:2] * _shift_rows(x, xh, 1) + cw[0:1] * _shift_rows(x, xh, 2))

    g = conv(wg_ref, cwg_ref, cbg_ref)
    up = conv(wu_ref, cwu_ref, cbu_ref)
    o_ref[...] = (_silu(g) * up).astype(o_ref.dtype)


def ffn_up(u, w_up, layer, conv_w, conv_b, seq_len, tm=1024, tn=512):
    t, d = u.shape
    dff = w_up.shape[2] // 2
    tm = _pick(seq_len, tm)
    tn = _pick(dff, tn)
    nj = dff // tn
    tm16 = tm // 16
    return pl.pallas_call(
        functools.partial(_ffn_up_kernel, tiles_per_seq=seq_len // tm),
        grid=(t // tm, nj),
        in_specs=[
            pl.BlockSpec((tm, d), lambda i, j: (i, 0)),
            pl.BlockSpec((16, d), lambda i, j: (jnp.maximum(i * tm16 - 1, 0), 0)),
            pl.BlockSpec((None, d, tn), lambda i, j: (layer, 0, j)),
            pl.BlockSpec((None, d, tn), lambda i, j: (layer, 0, j + nj)),
            pl.BlockSpec((FFN_CONV, tn), lambda i, j: (0, j)),
            pl.BlockSpec((FFN_CONV, tn), lambda i, j: (0, j + nj)),
            pl.BlockSpec((1, tn), lambda i, j: (0, j)),
            pl.BlockSpec((1, tn), lambda i, j: (0, j + nj)),
        ],
        out_specs=pl.BlockSpec((tm, tn), lambda i, j: (i, j)),
        out_shape=jax.ShapeDtypeStruct((t, dff), BF16),
        compiler_params=_cparams(("parallel", "parallel")),
        name="ffn_up",
    )(u, u, w_up, w_up, conv_w, conv_w, conv_b.reshape(1, -1), conv_b.reshape(1, -1))


def _hgrn_kernel(q_ref, f_ref, v_ref, g_ref, lb_ref, gn_ref, sel_ref, o_ref, st_ref, q_s, k_s, b_s, *, n_chunks):
    C, S = HGRN_CHUNK, HGRN_SUB
    nsub = C // S
    dk = q_ref.shape[-1]
    lb = lb_ref[0]
    one_m_lb = 1.0 - lb
    gn = gn_ref[...]
    tri = _tri(C, BF16)
    row = lax.broadcasted_iota(jnp.int32, (C, C), 0)
    col = lax.broadcasted_iota(jnp.int32, (C, C), 1)
    mask_diag = ((col // S) == (row // S)) & (col <= row)

    def prepare(ci, worst):
        r0 = pl.multiple_of(ci * C, C)
        sg = _sigmoid(f_ref[0, pl.ds(r0, C), :])
        q_s[pl.ds(r0, C), :] = _silu(q_ref[0, pl.ds(r0, C), :])
        k_s[pl.ds(r0, C), :] = one_m_lb * (1.0 - sg)
        b = _cumsum_rows(jnp.log2(lb + one_m_lb * sg), tri)
        b_s[pl.ds(r0, C), :] = b
        mid = b[C // 2 - 1:C // 2]
        return jnp.minimum(worst, jnp.minimum(mid, b[C - 1:C] - mid))

    worst = lax.fori_loop(0, n_chunks, prepare, jnp.zeros((1, dk), F32), unroll=True)
    assert HGRN_FAST_SUB == C // 2 and HGRN_FAST_CHUNK == 2 * C
    bounded = jnp.min(worst) > -HGRN_FACTOR_LIMIT

    st_ref[...] = jnp.zeros_like(st_ref)

    def scaled(r0, x, b, refs, sub, sign):
        ref = jnp.concatenate([jnp.zeros((sub, dk), F32) if r is None else jnp.broadcast_to(r, (sub, dk))
                               for r in refs], axis=0)
        return (x * jnp.exp2(sign * (b - ref))).astype(BF16)

    def finish(r0, rows, q, k, b, vb, a):
        st = st_ref[...]
        qe = (q * jnp.exp2(b)).astype(BF16)
        o = _dot(a.astype(BF16), vb) + _dot_nt(qe, st.astype(BF16))
        b_last = b[rows - 1:rows]
        kd = (k * jnp.exp2(b_last - b)).astype(BF16)
        st_ref[...] = st * jnp.exp2(b_last) + _dot_tn(vb, kd)
        gate = _silu(g_ref[0, pl.ds(r0, rows), :])
        y = o * lax.rsqrt(jnp.mean(o * o, axis=-1, keepdims=True) + NORM_EPS) * gn * gate
        o_ref[pl.ds(r0, rows), :] = y.astype(o_ref.dtype)

    CF, SF = HGRN_FAST_CHUNK, HGRN_FAST_SUB
    rowf = lax.broadcasted_iota(jnp.int32, (CF, CF), 0)
    colf = lax.broadcasted_iota(jnp.int32, (CF, CF), 1)

    def chunk_factorised(ci, carry):
        r0 = pl.multiple_of(ci * CF, CF)
        q = q_s[pl.ds(r0, CF), :]
        k = k_s[pl.ds(r0, CF), :]
        vb = v_ref[0, pl.ds(r0, CF), :].astype(BF16)
        b_top = b_s[pl.ds(r0, C), :]
        b = jnp.concatenate([b_top, b_s[pl.ds(r0 + C, C), :] + b_top[C - 1:C]], axis=0)
        refs = [None] + [b[SF * i - 1:SF * i] for i in range(1, CF // SF)]
        qs = scaled(r0, q, b, refs, SF, 1.0)
        blocks = []
        for i, ref in enumerate(refs):
            n = SF * (i + 1)
            kt = (k[:n] * jnp.exp2((0.0 if ref is None else ref) - b[:n])).astype(BF16)
            if n < CF:
                kt = jnp.concatenate([kt, jnp.zeros((CF - n, dk), BF16)], axis=0)
            blocks.append(_dot_nt(qs[SF * i:SF * (i + 1)], kt))
        a = jnp.where(colf <= rowf, jnp.concatenate(blocks, axis=0), 0.0)
        finish(r0, CF, q, k, b, vb, a)
        return carry

    def chunk_pairwise(ci, carry):
        r0 = pl.multiple_of(ci * C, C)
        q = q_s[pl.ds(r0, C), :]
        k = k_s[pl.ds(r0, C), :]
        b = b_s[pl.ds(r0, C), :]
        vb = v_ref[0, pl.ds(r0, C), :].astype(BF16)
        refs = [None] + [b[S * i - 1:S * i] for i in range(1, nsub)]
        qs = scaled(r0, q, b, refs, S, 1.0)
        blocks = [jnp.zeros((S, C), F32)]
        for i in range(1, nsub):
            n = S * i
            kt = (k[:n] * jnp.exp2(refs[i] - b[:n])).astype(BF16)
            kt = jnp.concatenate([kt, jnp.zeros((C - n, dk), BF16)], axis=0)
            blocks.append(_dot_nt(qs[n:n + S], kt))
        a_off = jnp.concatenate(blocks, axis=0)

        q8 = [q[8 * g:8 * g + 8] for g in range(C // 8)]
        b8 = [b[8 * g:8 * g + 8] for g in range(C // 8)]
        zero8 = jnp.zeros((8, dk), F32)
        pieces = []
        for jj in range(S):
            parts = []
            for g in range(C // 8):
                blk, half = divmod(g, S // 8)
                lo = 8 * half
                if lo + 8 <= jj:
                    parts.append(zero8)
                    continue
                r = r0 + S * blk + jj
                e = b8[g] - jnp.broadcast_to(b_s[pl.ds(r, 1), :], (8, dk))
                if lo < jj:
                    e = jnp.minimum(e, 0.0)
                parts.append(q8[g] * jnp.exp2(e) * jnp.broadcast_to(k_s[pl.ds(r, 1), :], (8, dk)))
            pieces.append(jnp.concatenate(parts, axis=0).astype(BF16))
        a_diag = _dot(jnp.concatenate(pieces, axis=1), sel_ref[...])[:, :C]
        finish(r0, C, q, k, b, vb, a_off + jnp.where(mask_diag, a_diag, 0.0))
        return carry

    @pl.when(bounded)
    def _():
        lax.fori_loop(0, n_chunks * C // CF, chunk_factorised, 0, unroll=True)

    @pl.when(jnp.logical_not(bounded))
    def _():
        lax.fori_loop(0, n_chunks, chunk_pairwise, 0, unroll=8)


def _hgrn_selector(dk):
    jj = jnp.arange(HGRN_SUB * dk) // dk
    c = jnp.arange(LANES)
    return ((c[None, :] % HGRN_SUB == jj[:, None]) & (c[None, :] < HGRN_CHUNK)).astype(BF16)


def hgrn_core(proj, lb, gn_w, batch, seq_len):
    nblk, t, dk = proj.shape
    heads = nblk // 4
    n_chunks = seq_len // HGRN_CHUNK

    def spec(section):
        return pl.BlockSpec((1, seq_len, dk), lambda b, h: (section * heads + h, b, 0))

    return pl.pallas_call(
        functools.partial(_hgrn_kernel, n_chunks=n_chunks),
        grid=(batch, heads),
        in_specs=[
            spec(0), spec(1), spec(2), spec(3),
            pl.BlockSpec((1, 1, dk), lambda b, h: (h, 0, 0)),
            pl.BlockSpec((1, dk), lambda b, h: (0, 0)),
            pl.BlockSpec((HGRN_SUB * dk, LANES), lambda b, h: (0, 0)),
        ],
        out_specs=pl.BlockSpec((seq_len, dk), lambda b, h: (b, h)),
        out_shape=jax.ShapeDtypeStruct((t, heads * dk), BF16),
        scratch_shapes=[
            pltpu.VMEM((dk, dk), F32),
            pltpu.VMEM((seq_len, dk), F32),
            pltpu.VMEM((seq_len, dk), F32),
            pltpu.VMEM((seq_len, dk), F32),
        ],
        compiler_params=_cparams(("parallel", "parallel")),
        name="hgrn_core",
    )(proj, proj, proj, proj, lb.reshape(heads, 1, dk), gn_w.reshape(1, dk), _hgrn_selector(dk))


LOG2E = 1.4426950408889634


CONV_PAD = 16


def _conv_rows(x, halo, w, bias, shift):
    ext = jnp.concatenate([jnp.zeros_like(halo), halo, x], axis=0)
    taps = jnp.concatenate([(w[t:t + 1] * ext).astype(BF16) for t in range(M_CONV)], axis=0)
    return _dot(shift, taps) + bias


def _conv_shift_matrix(rows):
    width = CONV_PAD + rows
    r = jnp.arange(rows)[:, None]
    c = jnp.arange(M_CONV * width)[None, :]
    tap, src = c // width, c % width
    return (src == CONV_PAD + r - (M_CONV - 1 - tap)).astype(BF16)


def _ssd_kernel(z_ref, x_ref, bm_ref, cm_ref, xh_ref, bh_ref, ch_ref, dt_ref,
                cwx_ref, cwb_ref, cwc_ref, cbx_ref, cbb_ref, cbc_ref,
                dtb_ref, alog_ref, dskip_ref, nw_ref, ex_ref, sh_ref, o_ref,
                xs_ref, xdt_ref, xw_ref, ea_ref, bs_ref, cs_ref, y_ref, hst_ref, acst_ref, acsr_ref,
                cbl_ref, cbh_ref, *, hpg):
    L = M_CHUNK
    n = M_D_STATE
    half = LANES // 2
    gw = hpg * M_HEADDIM
    npairs = hpg // 2
    first = pl.program_id(1) == 0

    @pl.when(first)
    def _():
        hst_ref[...] = jnp.zeros_like(hst_ref)

    def lane_of(rows):
        return lax.broadcasted_iota(jnp.int32, (rows, LANES), 1)

    def row_of(rows):
        return lax.broadcasted_iota(jnp.int32, (rows, LANES), 0)

    left = lane_of(L) < half
    left_h = lane_of(half) < half
    left1 = lane_of(1) < half
    mask_lo = row_of(L) >= lane_of(L) % half
    mask_hi = row_of(half) >= lane_of(half) % half

    dt = _softplus(dt_ref[...] + dtb_ref[...])
    a_neg = -jnp.exp(alog_ref[...])
    acs = _cumsum_rows(dt * a_neg, _tri(L, BF16)) * LOG2E
    acst = acs.T
    acst_ref[...] = acst
    acsr_ref[...] = pltpu.roll(acst, half, axis=1)

    def split_lanes(v):
        hi = v.astype(BF16).astype(F32)
        r1 = v - hi
        mid = r1.astype(BF16).astype(F32)
        lo = (r1 - mid).astype(BF16)
        first_cols = jnp.where(left, hi, pltpu.roll(mid, half, axis=1)).astype(BF16)
        return jnp.concatenate([first_cols, lo], axis=1)

    scal = jnp.concatenate([split_lanes(acs), split_lanes(dt)], axis=0)

    slab = gw

    def prep(s, carry):
        c0 = pl.multiple_of(s * slab, slab)
        halo = jnp.where(first, 0.0, xh_ref[:, pl.ds(c0, slab)])
        xs = _silu(_conv_rows(x_ref[:, pl.ds(c0, slab)], halo, cwx_ref[:, pl.ds(c0, slab)],
                              cbx_ref[:, pl.ds(c0, slab)], sh_ref[...]))
        xs_ref[:, pl.ds(c0, slab)] = xs
        e = _dot(scal, ex_ref[:, pl.ds(c0, slab)])
        ea = e[:L]
        ea_ref[:, pl.ds(c0, slab)] = ea
        xdt = xs * e[L:]
        xdt_ref[:, pl.ds(c0, slab)] = xdt.astype(BF16)
        xw_ref[:, pl.ds(c0, slab)] = (xdt * jnp.exp2(ea[L - 1:L] - ea)).astype(BF16)
        return carry

    lax.fori_loop(0, x_ref.shape[-1] // slab, prep, 0, unroll=True)

    def prep_bc(src_ref, halo_ref, w_ref, b_ref, dst_ref):
        def body(s, carry):
            c0 = pl.multiple_of(s * slab, slab)
            halo = jnp.where(first, 0.0, halo_ref[:, pl.ds(c0, slab)])
            dst_ref[:, pl.ds(c0, slab)] = _silu(_conv_rows(
                src_ref[:, pl.ds(c0, slab)], halo, w_ref[:, pl.ds(c0, slab)], b_ref[:, pl.ds(c0, slab)],
                sh_ref[...])).astype(dst_ref.dtype)
            return carry
        lax.fori_loop(0, src_ref.shape[-1] // slab, body, 0, unroll=2)

    prep_bc(bm_ref, bh_ref, cwb_ref, cbb_ref, bs_ref)
    prep_bc(cm_ref, ch_ref, cwc_ref, cbc_ref, cs_ref)

    def cb_prep(g, carry):
        g0 = pl.multiple_of(g * n, n)
        cb = _dot_nt(cs_ref[:, pl.ds(g0, n)], bs_ref[:, pl.ds(g0, n)].astype(BF16))
        cbr = pltpu.roll(cb, half, axis=1)
        cbl_ref[g] = jnp.where(left, cb, cbr)
        cbh_ref[g] = jnp.where(left, cbr, cb)[half:]
        return carry

    lax.fori_loop(0, M_GROUPS, cb_prep, 0, unroll=True)

    zero = jnp.zeros((half, LANES), BF16)

    def intra(g, carry):
        c0 = pl.multiple_of(g * gw, gw)
        cb_lo = cbl_ref[g]
        cb_hi = cbh_ref[g]
        for p in range(npairs):
            x0 = pl.multiple_of(c0 + p * LANES, LANES)
            h0 = g * hpg + 2 * p
            a_col = ea_ref[:, pl.ds(x0, LANES)]
            r0 = acst_ref[pl.ds(h0, 1), :]
            r1 = acst_ref[pl.ds(h0 + 1, 1), :]
            q0 = acsr_ref[pl.ds(h0, 1), :]
            q1 = acsr_ref[pl.ds(h0 + 1, 1), :]
            a_lo = jnp.where(left1, r0, q1)
            a_hi = jnp.where(left1, q0, r1)
            p_lo = (jnp.where(mask_lo, jnp.exp2(a_col - a_lo), 0.0) * cb_lo).astype(BF16)
            p_hi = (jnp.where(mask_hi, jnp.exp2(a_col[half:] - a_hi), 0.0) * cb_hi).astype(BF16)
            xd = xdt_ref[:, pl.ds(x0, LANES)]
            x_lo, x_hi = xd[:half], xd[half:]
            bd_lo = jnp.concatenate([jnp.where(left_h, x_lo, zero), jnp.where(left_h, zero, x_lo)], axis=0)
            bd_hi = jnp.concatenate([jnp.where(left_h, x_hi, zero), jnp.where(left_h, zero, x_hi)], axis=0)
            y = _dot(p_lo, bd_lo)
            y_bot = y[half:] + _dot(p_hi, bd_hi)
            y_ref[:, pl.ds(x0, LANES)] = jnp.concatenate([y[:half], y_bot], axis=0)
        return carry

    lax.fori_loop(0, M_GROUPS, intra, 0, unroll=True)

    def inter(g, carry):
        g0 = pl.multiple_of(g * n, n)
        c0 = pl.multiple_of(g * gw, gw)
        hg = hst_ref[:, pl.ds(c0, gw)]
        ea = ea_ref[:, pl.ds(c0, gw)]
        y = y_ref[:, pl.ds(c0, gw)] + _dot(cs_ref[:, pl.ds(g0, n)], hg.astype(BF16)) * jnp.exp2(ea)
        bgt = bs_ref[:, pl.ds(g0, n)].T.astype(BF16)
        hst_ref[:, pl.ds(c0, gw)] = hg * jnp.exp2(ea[L - 1:L]) + _dot(bgt, xw_ref[:, pl.ds(c0, gw)])
        y = y + xs_ref[:, pl.ds(c0, gw)] * dskip_ref[:, pl.ds(c0, gw)]
        y = y * _silu(z_ref[:, pl.ds(c0, gw)])
        y = y * lax.rsqrt(jnp.mean(y * y, axis=-1, keepdims=True) + NORM_EPS)
        o_ref[:, pl.ds(c0, gw)] = (y * nw_ref[:, pl.ds(c0, gw)]).astype(o_ref.dtype)
        return carry

    lax.fori_loop(0, M_GROUPS, inter, 0, unroll=True)


def _head_expander(heads, d_inner):
    half = LANES // 2
    r = jnp.arange(2 * LANES)
    c = jnp.arange(d_inner)
    head_of_row = jnp.where(r < 3 * half, r % half, -1)
    return ((head_of_row[:, None] == (c // M_HEADDIM)[None, :]) & (head_of_row[:, None] < heads)).astype(BF16)


def ssd_core(zx, dt, conv_w, conv_b, dt_bias, a_log, d_skip, norm_w, batch, seq_len):
    t = zx.shape[0]
    gn = M_GROUPS * M_D_STATE
    d_inner = (zx.shape[1] - 2 * gn) // 2
    heads = d_inner // M_HEADDIM
    hpg = heads // M_GROUPS
    L = M_CHUNK
    nc = seq_len // L
    gw = hpg * M_HEADDIM
    assert d_inner % gn == 0 and heads <= LANES // 2 and hpg % 2 == 0 and L == LANES and 2 * M_HEADDIM == LANES
    assert gn % gw == 0
    r = d_inner // gn
    l8 = L // 8

    def rows(b, c):
        return b * nc + c

    def halo_rows(b, c):
        return jnp.maximum((b * nc + c) * l8 - 1, 0)

    pad = LANES - heads
    dtb = jnp.pad(dt_bias.astype(F32), (0, pad)).reshape(1, LANES)
    alog = jnp.pad(a_log.astype(F32), (0, pad)).reshape(1, LANES)
    dskip = jnp.repeat(d_skip.astype(F32), M_HEADDIM).reshape(1, d_inner)
    cb2 = conv_b.reshape(1, -1)

    in_specs = [
        pl.BlockSpec((L, d_inner), lambda b, c: (rows(b, c), 0)),
        pl.BlockSpec((L, d_inner), lambda b, c: (rows(b, c), 1)),
        pl.BlockSpec((L, gn), lambda b, c: (rows(b, c), 2 * r)),
        pl.BlockSpec((L, gn), lambda b, c: (rows(b, c), 2 * r + 1)),
        pl.BlockSpec((8, d_inner), lambda b, c: (halo_rows(b, c), 1)),
        pl.BlockSpec((8, gn), lambda b, c: (halo_rows(b, c), 2 * r)),
        pl.BlockSpec((8, gn), lambda b, c: (halo_rows(b, c), 2 * r + 1)),
        pl.BlockSpec((L, LANES), lambda b, c: (rows(b, c), 0)),
        pl.BlockSpec((M_CONV, d_inner), lambda b, c: (0, 0)),
        pl.BlockSpec((M_CONV, gn), lambda b, c: (0, r)),
        pl.BlockSpec((M_CONV, gn), lambda b, c: (0, r + 1)),
        pl.BlockSpec((1, d_inner), lambda b, c: (0, 0)),
        pl.BlockSpec((1, gn), lambda b, c: (0, r)),
        pl.BlockSpec((1, gn), lambda b, c: (0, r + 1)),
        pl.BlockSpec((1, LANES), lambda b, c: (0, 0)),
        pl.BlockSpec((1, LANES), lambda b, c: (0, 0)),
        pl.BlockSpec((1, d_inner), lambda b, c: (0, 0)),
        pl.BlockSpec((1, d_inner), lambda b, c: (0, 0)),
        pl.BlockSpec((2 * LANES, d_inner), lambda b, c: (0, 0)),
        pl.BlockSpec((L, M_CONV * (CONV_PAD + L)), lambda b, c: (0, 0)),
    ]
    return pl.pallas_call(
        functools.partial(_ssd_kernel, hpg=hpg),
        grid=(batch, nc),
        in_specs=in_specs,
        out_specs=pl.BlockSpec((L, d_inner), lambda b, c: (rows(b, c), 0)),
        out_shape=jax.ShapeDtypeStruct((t, d_inner), BF16),
        scratch_shapes=[
            pltpu.VMEM((L, d_inner), F32),
            pltpu.VMEM((L, d_inner), BF16),
            pltpu.VMEM((L, d_inner), BF16),
            pltpu.VMEM((L, d_inner), F32),
            pltpu.VMEM((L, gn), F32),
            pltpu.VMEM((L, gn), BF16),
            pltpu.VMEM((L, d_inner), F32),
            pltpu.VMEM((M_D_STATE, d_inner), F32),
            pltpu.VMEM((LANES, L), F32),
            pltpu.VMEM((LANES, L), F32),
            pltpu.VMEM((M_GROUPS, L, LANES), F32),
            pltpu.VMEM((M_GROUPS, L // 2, LANES), F32),
        ],
        compiler_params=_cparams(("parallel", "arbitrary")),
        name="ssd_core",
    )(zx, zx, zx, zx, zx, zx, zx, dt, conv_w, conv_w, conv_w, cb2, cb2, cb2, dtb, alog, dskip,
      norm_w.reshape(1, d_inner), _head_expander(heads, d_inner), _conv_shift_matrix(L))


def kernel(x, mix_norm, ffn_norm, final_norm, hgrn_w_in, hgrn_lb_logits, hgrn_gnorm, hgrn_w_out,
           m_w_in, m_conv_w, m_conv_b, m_dt_bias, m_A_log, m_D, m_norm, m_w_out,
           f_w_up, f_conv_w, f_conv_b, f_w_down):
    batch, seq_len, d = x.shape
    depth = mix_norm.shape[0]
    t = batch * seq_len

    lb_p = jax.nn.softmax(hgrn_lb_logits.astype(F32), axis=0)
    lower_bounds = jnp.cumsum(lb_p, axis=0) - lb_p[0]

    d_inner = m_w_out.shape[1]
    m_heads = d_inner // M_HEADDIM
    zx_dim = m_w_in.shape[2] - m_heads

    hgrn_w_in_b, hgrn_w_out_b = hgrn_w_in.astype(BF16), hgrn_w_out.astype(BF16)
    m_w_in_b, m_w_out_b = m_w_in.astype(BF16), m_w_out.astype(BF16)
    f_w_up_b, f_w_down_b = f_w_up.astype(BF16), f_w_down.astype(BF16)
    w_dt_b = jnp.pad(m_w_in[:, :, zx_dim:], ((0, 0), (0, 0), (0, LANES - m_heads))).astype(BF16)

    h = x.reshape(t, d)
    u = rmsnorm(h, mix_norm[0], BF16)
    out = None
    for i in range(depth):
        j = i // 2
        if i % 2 == 0:
            proj = matmul_headmajor(u, hgrn_w_in_b, j)
            y = hgrn_core(proj, lower_bounds[j], hgrn_gnorm[j], batch, seq_len)
            h, u = matmul_res_norm(y, hgrn_w_out_b, j, h, ffn_norm[i], BF16)
        else:
            zx = matmul(u, m_w_in_b, j, zx_dim)
            dt = matmul(u, w_dt_b, j, LANES, tn=LANES)
            y = ssd_core(zx, dt, m_conv_w[j], m_conv_b[j], m_dt_bias[j], m_A_log[j], m_D[j], m_norm[j],
                         batch, seq_len)
            h, u = matmul_res_norm(y, m_w_out_b, j, h, ffn_norm[i], BF16)
        a = ffn_up(u, f_w_up_b, i, f_conv_w[i], f_conv_b[i], seq_len)
        if i + 1 < depth:
            h, u = matmul_res_norm(a, f_w_down_b, i, h, mix_norm[i + 1], BF16)
        else:
            h, out = matmul_res_norm(a, f_w_down_b, i, h, final_norm, F32)
    return out.reshape(batch, seq_len, d)
```

```python
import functools

import jax
import jax.numpy as jnp
from jax import lax
from jax.experimental import pallas as pl
from jax.experimental.pallas import tpu as pltpu

F32 = jnp.float32
BF16 = jnp.bfloat16

NORM_EPS = 1e-5
LANES = 128
HGRN_CHUNK = 64
HGRN_SUB = 16
HGRN_FAST_CHUNK = 128
HGRN_FAST_SUB = 32
HGRN_FACTOR_LIMIT = 100.0
M_HEADDIM = 64
M_GROUPS = 8
M_D_STATE = 128
M_CONV = 4
M_CHUNK = 128
FFN_CONV = 3
VMEM_LIMIT = 56 * 1024 * 1024


def _cparams(sem):
    return pltpu.CompilerParams(dimension_semantics=sem, vmem_limit_bytes=VMEM_LIMIT)


def _pick(n, pref):
    t = min(pref, n)
    while n % t:
        t -= LANES if t > LANES else 8
    return t


def _dot(a, b):
    return jnp.dot(a, b, preferred_element_type=F32)


def _dot_nt(a, b):
    return lax.dot_general(a, b, (((1,), (1,)), ((), ())), preferred_element_type=F32)


def _dot_tn(a, b):
    return lax.dot_general(a, b, (((0,), (0,)), ((), ())), preferred_element_type=F32)


def _split3(x):
    hi = x.astype(BF16)
    r1 = x - hi.astype(F32)
    mid = r1.astype(BF16)
    lo = (r1 - mid.astype(F32)).astype(BF16)
    return hi, mid, lo


def _cumsum_rows(x, tri):
    hi, mid, lo = _split3(x)
    return _dot(tri, hi) + _dot(tri, mid) + _dot(tri, lo)


def _tri(n, dtype):
    r = lax.broadcasted_iota(jnp.int32, (n, n), 0)
    c = lax.broadcasted_iota(jnp.int32, (n, n), 1)
    return (r >= c).astype(dtype)


def _sigmoid(x):
    return 1.0 / (1.0 + jnp.exp(-x))


def _silu(x):
    return x * _sigmoid(x)


def _softplus(x):
    return jnp.maximum(x, 0.0) + jnp.log(1.0 + jnp.exp(-jnp.abs(x)))


def _rmsnorm_kernel(x_ref, w_ref, o_ref):
    x = x_ref[...]
    y = x * lax.rsqrt(jnp.mean(x * x, axis=-1, keepdims=True) + NORM_EPS)
    o_ref[...] = (y * w_ref[...]).astype(o_ref.dtype)


def rmsnorm(x, w, out_dtype):
    t, d = x.shape
    tm = _pick(t, 512)
    return pl.pallas_call(
        _rmsnorm_kernel,
        grid=(t // tm,),
        in_specs=[pl.BlockSpec((tm, d), lambda i: (i, 0)), pl.BlockSpec((1, d), lambda i: (0, 0))],
        out_specs=pl.BlockSpec((tm, d), lambda i: (i, 0)),
        out_shape=jax.ShapeDtypeStruct((t, d), out_dtype),
        compiler_params=_cparams(("parallel",)),
        name="rmsnorm",
    )(x, w.reshape(1, d))


def _cast_kernel(x_ref, o_ref):
    o_ref[...] = x_ref[...].astype(o_ref.dtype)


def cast_columns(w, ncols, dtype, tr=256):
    layers, k, _ = w.shape
    tr = _pick(k, tr)
    spec = pl.BlockSpec((None, tr, ncols), lambda l, r: (l, r, 0))
    return pl.pallas_call(
        _cast_kernel,
        grid=(layers, k // tr),
        in_specs=[spec],
        out_specs=spec,
        out_shape=jax.ShapeDtypeStruct((layers, k, ncols), dtype),
        compiler_params=_cparams(("parallel", "parallel")),
        name="cast_columns",
    )(w)


def _mm_kernel(x_ref, w_ref, o_ref):
    o_ref[...] = _dot(x_ref[...], w_ref[...]).astype(o_ref.dtype)


def matmul(x, w, layer, n, out_dtype=F32, tm=2048, tn=1024):
    t, k = x.shape
    tm, tn = _pick(t, tm), _pick(n, tn)
    return pl.pallas_call(
        _mm_kernel,
        grid=(t // tm, n // tn),
        in_specs=[pl.BlockSpec((tm, k), lambda i, j: (i, 0)),
                  pl.BlockSpec((None, k, tn), lambda i, j: (layer, 0, j))],
        out_specs=pl.BlockSpec((tm, tn), lambda i, j: (i, j)),
        out_shape=jax.ShapeDtypeStruct((t, n), out_dtype),
        compiler_params=_cparams(("parallel", "parallel")),
        name="matmul",
    )(x, w)


def _mm_heads_kernel(x_ref, w_ref, o_ref, *, nblk):
    acc = _dot(x_ref[...], w_ref[...])
    for c in range(nblk):
        o_ref[c] = acc[:, c * LANES:(c + 1) * LANES]


def matmul_headmajor(x, w, layer, tm=2048, tn=1024):
    t, k = x.shape
    n = w.shape[2]
    tm, tn = _pick(t, tm), _pick(n, tn)
    nblk = tn // LANES
    return pl.pallas_call(
        functools.partial(_mm_heads_kernel, nblk=nblk),
        grid=(t // tm, n // tn),
        in_specs=[pl.BlockSpec((tm, k), lambda i, j: (i, 0)),
                  pl.BlockSpec((None, k, tn), lambda i, j: (layer, 0, j))],
        out_specs=pl.BlockSpec((nblk, tm, LANES), lambda i, j: (j, i, 0)),
        out_shape=jax.ShapeDtypeStruct((n // LANES, t, LANES), F32),
        compiler_params=_cparams(("parallel", "parallel")),
        name="matmul_headmajor",
    )(x, w)


def _mm_res_norm_kernel(x_ref, w_ref, h_ref, nw_ref, h_out_ref, u_out_ref):
    h = h_ref[...] + _dot(x_ref[...], w_ref[...])
    h_out_ref[...] = h
    y = h * lax.rsqrt(jnp.mean(h * h, axis=-1, keepdims=True) + NORM_EPS)
    u_out_ref[...] = (y * nw_ref[...]).astype(u_out_ref.dtype)


def matmul_res_norm(x, w, layer, h, norm_w, u_dtype):
    t, kdim = x.shape
    d = w.shape[2]
    row_bytes = 2 * 2 * kdim + (4 * 4 + 2 * jnp.dtype(u_dtype).itemsize + 4) * d
    budget = VMEM_LIMIT - 2 * kdim * d - (6 << 20)
    tm = 1024
    while tm * row_bytes > budget or t % tm:
        tm //= 2
    return pl.pallas_call(
        _mm_res_norm_kernel,
        grid=(t // tm,),
        in_specs=[
            pl.BlockSpec((tm, kdim), lambda i: (i, 0)),
            pl.BlockSpec((None, kdim, d), lambda i: (layer, 0, 0), pipeline_mode=pl.Buffered(1)),
            pl.BlockSpec((tm, d), lambda i: (i, 0)),
            pl.BlockSpec((1, d), lambda i: (0, 0)),
        ],
        out_specs=[pl.BlockSpec((tm, d), lambda i: (i, 0)), pl.BlockSpec((tm, d), lambda i: (i, 0))],
        out_shape=[jax.ShapeDtypeStruct((t, d), F32), jax.ShapeDtypeStruct((t, d), u_dtype)],
        compiler_params=_cparams(("parallel",)),
        name="matmul_res_norm",
    )(x, w, h, norm_w.reshape(1, d))


def _shift_rows(x, halo, s):
    rolled = pltpu.roll(x, s, axis=0)
    row = lax.broadcasted_iota(jnp.int32, halo.shape, 0)
    top = jnp.where(row < s, pltpu.roll(halo, s, axis=0), rolled[:8])
    return jnp.concatenate([top, rolled[8:]], axis=0)


def _ffn_up_kernel(u_ref, uh_ref, wg_ref, wu_ref, cwg_ref, cwu_ref, cbg_ref, cbu_ref, o_ref, *, tiles_per_seq):
    i = pl.program_id(0)
    first = (i % tiles_per_seq) == 0
    uh = jnp.where(first, jnp.zeros_like(uh_ref[...]), uh_ref[...])
    lhs = jnp.concatenate([uh, u_ref[...]], axis=0)

    def conv(w_ref, cw_ref, cb_ref):
        xe = _dot(lhs, w_ref[...])
        x, xh = xe[16:], xe[8:16]
        cw = cw_ref[...]
        return (cb_ref[...] + cw[2:3] * x + cw[1:2] * _shift_rows(x, xh, 1) + cw[0:1] * _shift_rows(x, xh, 2))

    g = conv(wg_ref, cwg_ref, cbg_ref)
    up = conv(wu_ref, cwu_ref, cbu_ref)
    o_ref[...] = (_silu(g) * up).astype(o_ref.dtype)


def ffn_up(u, w_up, layer, conv_w, conv_b, seq_len, tm=1024, tn=512):
    t, d = u.shape
    dff = w_up.shape[2] // 2
    tm = _pick(seq_len, tm)
    tn = _pick(dff, tn)
    nj = dff // tn
    tm16 = tm // 16
    return pl.pallas_call(
        functools.partial(_ffn_up_kernel, tiles_per_seq=seq_len // tm),
        grid=(t // tm, nj),
        in_specs=[
            pl.BlockSpec((tm, d), lambda i, j: (i, 0)),
            pl.BlockSpec((16, d), lambda i, j: (jnp.maximum(i * tm16 - 1, 0), 0)),
            pl.BlockSpec((None, d, tn), lambda i, j: (layer, 0, j)),
            pl.BlockSpec((None, d, tn), lambda i, j: (layer, 0, j + nj)),
            pl.BlockSpec((FFN_CONV, tn), lambda i, j: (0, j)),
            pl.BlockSpec((FFN_CONV, tn), lambda i, j: (0, j + nj)),
            pl.BlockSpec((1, tn), lambda i, j: (0, j)),
            pl.BlockSpec((1, tn), lambda i, j: (0, j + nj)),
        ],
        out_specs=pl.BlockSpec((tm, tn), lambda i, j: (i, j)),
        out_shape=jax.ShapeDtypeStruct((t, dff), BF16),
        compiler_params=_cparams(("parallel", "parallel")),
        name="ffn_up",
    )(u, u, w_up, w_up, conv_w, conv_w, conv_b.reshape(1, -1), conv_b.reshape(1, -1))


def _hgrn_kernel(q_ref, f_ref, v_ref, g_ref, lb_ref, gn_ref, sel_ref, o_ref, st_ref, q_s, k_s, b_s, *, n_chunks):
    C, S = HGRN_CHUNK, HGRN_SUB
    nsub = C // S
    dk = q_ref.shape[-1]
    lb = lb_ref[0]
    one_m_lb = 1.0 - lb
    gn = gn_ref[...]
    tri = _tri(C, BF16)
    row = lax.broadcasted_iota(jnp.int32, (C, C), 0)
    col = lax.broadcasted_iota(jnp.int32, (C, C), 1)
    mask_diag = ((col // S) == (row // S)) & (col <= row)

    def prepare(ci, worst):
        r0 = pl.multiple_of(ci * C, C)
        sg = _sigmoid(f_ref[0, pl.ds(r0, C), :])
        q_s[pl.ds(r0, C), :] = _silu(q_ref[0, pl.ds(r0, C), :])
        k_s[pl.ds(r0, C), :] = one_m_lb * (1.0 - sg)
        b = _cumsum_rows(jnp.log2(lb + one_m_lb * sg), tri)
        b_s[pl.ds(r0, C), :] = b
        mid = b[C // 2 - 1:C // 2]
        return jnp.minimum(worst, jnp.minimum(mid, b[C - 1:C] - mid))

    worst = lax.fori_loop(0, n_chunks, prepare, jnp.zeros((1, dk), F32), unroll=True)
    assert HGRN_FAST_SUB == C // 2 and HGRN_FAST_CHUNK == 2 * C
    bounded = jnp.min(worst) > -HGRN_FACTOR_LIMIT

    st_ref[...] = jnp.zeros_like(st_ref)

    def scaled_queries(q, b, refs, sub):
        ref = jnp.concatenate([jnp.zeros((sub, dk), F32) if r is None else jnp.broadcast_to(r, (sub, dk))
                               for r in refs], axis=0)
        return (q * jnp.exp2(b - ref)).astype(BF16)

    def finish(r0, rows, q, k, b, vb, a):
        st = st_ref[...]
        qe = (q * jnp.exp2(b)).astype(BF16)
        o = _dot(a.astype(BF16), vb) + _dot_nt(qe, st.astype(BF16))
        b_last = b[rows - 1:rows]
        kd = (k * jnp.exp2(b_last - b)).astype(BF16)
        st_ref[...] = st * jnp.exp2(b_last) + _dot_tn(vb, kd)
        gate = _silu(g_ref[0, pl.ds(r0, rows), :])
        y = o * lax.rsqrt(jnp.mean(o * o, axis=-1, keepdims=True) + NORM_EPS) * gn * gate
        o_ref[pl.ds(r0, rows), :] = y.astype(o_ref.dtype)

    CF, SF = HGRN_FAST_CHUNK, HGRN_FAST_SUB
    rowf = lax.broadcasted_iota(jnp.int32, (CF, CF), 0)
    colf = lax.broadcasted_iota(jnp.int32, (CF, CF), 1)

    def chunk_factorised(ci, carry):
        r0 = pl.multiple_of(ci * CF, CF)
        q = q_s[pl.ds(r0, CF), :]
        k = k_s[pl.ds(r0, CF), :]
        vb = v_ref[0, pl.ds(r0, CF), :].astype(BF16)
        b_top = b_s[pl.ds(r0, C), :]
        b = jnp.concatenate([b_top, b_s[pl.ds(r0 + C, C), :] + b_top[C - 1:C]], axis=0)
        refs = [None] + [b[SF * i - 1:SF * i] for i in range(1, CF // SF)]
        qs = scaled_queries(q, b, refs, SF)
        blocks = []
        for i, ref in enumerate(refs):
            n = SF * (i + 1)
            kt = (k[:n] * jnp.exp2((0.0 if ref is None else ref) - b[:n])).astype(BF16)
            if n < CF:
                kt = jnp.concatenate([kt, jnp.zeros((CF - n, dk), BF16)], axis=0)
            blocks.append(_dot_nt(qs[SF * i:SF * (i + 1)], kt))
        a = jnp.where(colf <= rowf, jnp.concatenate(blocks, axis=0), 0.0)
        finish(r0, CF, q, k, b, vb, a)
        return carry

    def chunk_pairwise(ci, carry):
        r0 = pl.multiple_of(ci * C, C)
        q = q_s[pl.ds(r0, C), :]
        k = k_s[pl.ds(r0, C), :]
        b = b_s[pl.ds(r0, C), :]
        vb = v_ref[0, pl.ds(r0, C), :].astype(BF16)
        refs = [None] + [b[S * i - 1:S * i] for i in range(1, nsub)]
        qs = scaled_queries(q, b, refs, S)
        blocks = [jnp.zeros((S, C), F32)]
        for i in range(1, nsub):
            n = S * i
            kt = (k[:n] * jnp.exp2(refs[i] - b[:n])).astype(BF16)
            kt = jnp.concatenate([kt, jnp.zeros((C - n, dk), BF16)], axis=0)
            blocks.append(_dot_nt(qs[n:n + S], kt))
        a_off = jnp.concatenate(blocks, axis=0)

        q8 = [q[8 * g:8 * g + 8] for g in range(C // 8)]
        b8 = [b[8 * g:8 * g + 8] for g in range(C // 8)]
        zero8 = jnp.zeros((8, dk), F32)
        pieces = []
        for jj in range(S):
            parts = []
            for g in range(C // 8):
                blk, half = divmod(g, S // 8)
                lo = 8 * half
                if lo + 8 <= jj:
                    parts.append(zero8)
                    continue
                r = r0 + S * blk + jj
                e = b8[g] - jnp.broadcast_to(b_s[pl.ds(r, 1), :], (8, dk))
                if lo < jj:
                    e = jnp.minimum(e, 0.0)
                parts.append(q8[g] * jnp.exp2(e) * jnp.broadcast_to(k_s[pl.ds(r, 1), :], (8, dk)))
            pieces.append(jnp.concatenate(parts, axis=0).astype(BF16))
        a_diag = _dot(jnp.concatenate(pieces, axis=1), sel_ref[...])[:, :C]
        finish(r0, C, q, k, b, vb, a_off + jnp.where(mask_diag, a_diag, 0.0))
        return carry

    @pl.when(bounded)
    def _():
        lax.fori_loop(0, n_chunks * C // CF, chunk_factorised, 0, unroll=True)

    @pl.when(jnp.logical_not(bounded))
    def _():
        lax.fori_loop(0, n_chunks, chunk_pairwise, 0, unroll=8)


def _hgrn_selector(dk):
    jj = jnp.arange(HGRN_SUB * dk) // dk
    c = jnp.arange(LANES)
    return ((c[None, :] % HGRN_SUB == jj[:, None]) & (c[None, :] < HGRN_CHUNK)).astype(BF16)


def hgrn_core(proj, lb, gn_w, batch, seq_len):
    nblk, t, dk = proj.shape
    heads = nblk // 4
    n_chunks = seq_len // HGRN_CHUNK

    def spec(section):
        return pl.BlockSpec((1, seq_len, dk), lambda b, h: (section * heads + h, b, 0))

    return pl.pallas_call(
        functools.partial(_hgrn_kernel, n_chunks=n_chunks),
        grid=(batch, heads),
        in_specs=[
            spec(0), spec(1), spec(2), spec(3),
            pl.BlockSpec((1, 1, dk), lambda b, h: (h, 0, 0)),
            pl.BlockSpec((1, dk), lambda b, h: (0, 0)),
            pl.BlockSpec((HGRN_SUB * dk, LANES), lambda b, h: (0, 0)),
        ],
        out_specs=pl.BlockSpec((seq_len, dk), lambda b, h: (b, h)),
        out_shape=jax.ShapeDtypeStruct((t, heads * dk), BF16),
        scratch_shapes=[
            pltpu.VMEM((dk, dk), F32),
            pltpu.VMEM((seq_len, dk), F32),
            pltpu.VMEM((seq_len, dk), F32),
            pltpu.VMEM((seq_len, dk), F32),
        ],
        compiler_params=_cparams(("parallel", "parallel")),
        name="hgrn_core",
    )(proj, proj, proj, proj, lb.reshape(heads, 1, dk), gn_w.reshape(1, dk), _hgrn_selector(dk))


LOG2E = 1.4426950408889634


CONV_PAD = 16


def _conv_rows(x, halo, w, bias, shift):
    ext = jnp.concatenate([jnp.zeros_like(halo), halo, x], axis=0)
    taps = jnp.concatenate([(w[t:t + 1] * ext).astype(BF16) for t in range(M_CONV)], axis=0)
    return _dot(shift, taps) + bias


def _conv_shift_matrix(rows):
    width = CONV_PAD + rows
    r = jnp.arange(rows)[:, None]
    c = jnp.arange(M_CONV * width)[None, :]
    tap, src = c // width, c % width
    return (src == CONV_PAD + r - (M_CONV - 1 - tap)).astype(BF16)


def _ssd_kernel(z_ref, x_ref, bm_ref, cm_ref, xh_ref, bh_ref, ch_ref, dt_ref,
                cwx_ref, cwb_ref, cwc_ref, cbx_ref, cbb_ref, cbc_ref,
                dtb_ref, alog_ref, dskip_ref, nw_ref, ex_ref, sh_ref, o_ref,
                xs_ref, xdt_ref, xw_ref, ea_ref, bs_ref, cs_ref, y_ref, hst_ref, acst_ref, acsr_ref,
                cbl_ref, cbh_ref, *, hpg):
    L = M_CHUNK
    n = M_D_STATE
    half = LANES // 2
    gw = hpg * M_HEADDIM
    npairs = hpg // 2
    first = pl.program_id(1) == 0

    @pl.when(first)
    def _():
        hst_ref[...] = jnp.zeros_like(hst_ref)

    def lane_of(rows):
        return lax.broadcasted_iota(jnp.int32, (rows, LANES), 1)

    def row_of(rows):
        return lax.broadcasted_iota(jnp.int32, (rows, LANES), 0)

    left = lane_of(L) < half
    left_h = lane_of(half) < half
    left1 = lane_of(1) < half
    mask_lo = row_of(L) >= lane_of(L) % half
    mask_hi = row_of(half) >= lane_of(half) % half

    dt = _softplus(dt_ref[...] + dtb_ref[...])
    a_neg = -jnp.exp(alog_ref[...])
    acs = _cumsum_rows(dt * a_neg, _tri(L, BF16)) * LOG2E
    acst = acs.T
    acst_ref[...] = acst
    acsr_ref[...] = pltpu.roll(acst, half, axis=1)

    def split_lanes(v):
        hi = v.astype(BF16).astype(F32)
        r1 = v - hi
        mid = r1.astype(BF16).astype(F32)
        lo = (r1 - mid).astype(BF16)
        first_cols = jnp.where(left, hi, pltpu.roll(mid, half, axis=1)).astype(BF16)
        return jnp.concatenate([first_cols, lo], axis=1)

    scal = jnp.concatenate([split_lanes(acs), split_lanes(dt)], axis=0)

    slab = gw

    def prep(s, carry):
        c0 = pl.multiple_of(s * slab, slab)
        halo = jnp.where(first, 0.0, xh_ref[:, pl.ds(c0, slab)])
        xs = _silu(_conv_rows(x_ref[:, pl.ds(c0, slab)], halo, cwx_ref[:, pl.ds(c0, slab)],
                              cbx_ref[:, pl.ds(c0, slab)], sh_ref[...]))
        xs_ref[:, pl.ds(c0, slab)] = xs
        e = _dot(scal, ex_ref[:, pl.ds(c0, slab)])
        ea = e[:L]
        ea_ref[:, pl.ds(c0, slab)] = ea
        xdt = xs * e[L:]
        xdt_ref[:, pl.ds(c0, slab)] = xdt.astype(BF16)
        xw_ref[:, pl.ds(c0, slab)] = (xdt * jnp.exp2(ea[L - 1:L] - ea)).astype(BF16)
        return carry

    lax.fori_loop(0, x_ref.shape[-1] // slab, prep, 0, unroll=True)

    def prep_bc(src_ref, halo_ref, w_ref, b_ref, dst_ref):
        def body(s, carry):
            c0 = pl.multiple_of(s * slab, slab)
            halo = jnp.where(first, 0.0, halo_ref[:, pl.ds(c0, slab)])
            dst_ref[:, pl.ds(c0, slab)] = _silu(_conv_rows(
                src_ref[:, pl.ds(c0, slab)], halo, w_ref[:, pl.ds(c0, slab)], b_ref[:, pl.ds(c0, slab)],
                sh_ref[...])).astype(dst_ref.dtype)
            return carry
        lax.fori_loop(0, src_ref.shape[-1] // slab, body, 0, unroll=2)

    prep_bc(bm_ref, bh_ref, cwb_ref, cbb_ref, bs_ref)
    prep_bc(cm_ref, ch_ref, cwc_ref, cbc_ref, cs_ref)

    def cb_prep(g, carry):
        g0 = pl.multiple_of(g * n, n)
        cb = _dot_nt(cs_ref[:, pl.ds(g0, n)], bs_ref[:, pl.ds(g0, n)].astype(BF16))
        cbr = pltpu.roll(cb, half, axis=1)
        cbl_ref[g] = jnp.where(left, cb, cbr)
        cbh_ref[g] = jnp.where(left, cbr, cb)[half:]
        return carry

    lax.fori_loop(0, M_GROUPS, cb_prep, 0, unroll=True)

    zero = jnp.zeros((half, LANES), BF16)

    def intra(g, carry):
        c0 = pl.multiple_of(g * gw, gw)
        cb_lo = cbl_ref[g]
        cb_hi = cbh_ref[g]
        for p in range(npairs):
            x0 = pl.multiple_of(c0 + p * LANES, LANES)
            h0 = g * hpg + 2 * p
            a_col = ea_ref[:, pl.ds(x0, LANES)]
            r0 = acst_ref[pl.ds(h0, 1), :]
            r1 = acst_ref[pl.ds(h0 + 1, 1), :]
            q0 = acsr_ref[pl.ds(h0, 1), :]
            q1 = acsr_ref[pl.ds(h0 + 1, 1), :]
            a_lo = jnp.where(left1, r0, q1)
            a_hi = jnp.where(left1, q0, r1)
            p_lo = (jnp.where(mask_lo, jnp.exp2(a_col - a_lo), 0.0) * cb_lo).astype(BF16)
            p_hi = (jnp.where(mask_hi, jnp.exp2(a_col[half:] - a_hi), 0.0) * cb_hi).astype(BF16)
            xd = xdt_ref[:, pl.ds(x0, LANES)]
            x_lo, x_hi = xd[:half], xd[half:]
            bd_lo = jnp.concatenate([jnp.where(left_h, x_lo, zero), jnp.where(left_h, zero, x_lo)], axis=0)
            bd_hi = jnp.concatenate([jnp.where(left_h, x_hi, zero), jnp.where(left_h, zero, x_hi)], axis=0)
            y = _dot(p_lo, bd_lo)
            y_bot = y[half:] + _dot(p_hi, bd_hi)
            y_ref[:, pl.ds(x0, LANES)] = jnp.concatenate([y[:half], y_bot], axis=0)
        return carry

    lax.fori_loop(0, M_GROUPS, intra, 0, unroll=True)

    def inter(g, carry):
        g0 = pl.multiple_of(g * n, n)
        c0 = pl.multiple_of(g * gw, gw)
        hg = hst_ref[:, pl.ds(c0, gw)]
        ea = ea_ref[:, pl.ds(c0, gw)]
        y = y_ref[:, pl.ds(c0, gw)] + _dot(cs_ref[:, pl.ds(g0, n)], hg.astype(BF16)) * jnp.exp2(ea)
        bgt = bs_ref[:, pl.ds(g0, n)].T.astype(BF16)
        hst_ref[:, pl.ds(c0, gw)] = hg * jnp.exp2(ea[L - 1:L]) + _dot(bgt, xw_ref[:, pl.ds(c0, gw)])
        y = y + xs_ref[:, pl.ds(c0, gw)] * dskip_ref[:, pl.ds(c0, gw)]
        y = y * _silu(z_ref[:, pl.ds(c0, gw)])
        y = y * lax.rsqrt(jnp.mean(y * y, axis=-1, keepdims=True) + NORM_EPS)
        o_ref[:, pl.ds(c0, gw)] = (y * nw_ref[:, pl.ds(c0, gw)]).astype(o_ref.dtype)
        return carry

    lax.fori_loop(0, M_GROUPS, inter, 0, unroll=True)


def _head_expander(heads, d_inner):
    half = LANES // 2
    r = jnp.arange(2 * LANES)
    c = jnp.arange(d_inner)
    head_of_row = jnp.where(r < 3 * half, r % half, -1)
    return ((head_of_row[:, None] == (c // M_HEADDIM)[None, :]) & (head_of_row[:, None] < heads)).astype(BF16)


def ssd_core(zx, dt, conv_w, conv_b, dt_bias, a_log, d_skip, norm_w, batch, seq_len):
    t = zx.shape[0]
    gn = M_GROUPS * M_D_STATE
    d_inner = (zx.shape[1] - 2 * gn) // 2
    heads = d_inner // M_HEADDIM
    hpg = heads // M_GROUPS
    L = M_CHUNK
    nc = seq_len // L
    gw = hpg * M_HEADDIM
    assert d_inner % gn == 0 and heads <= LANES // 2 and hpg % 2 == 0 and L == LANES and 2 * M_HEADDIM == LANES
    assert gn % gw == 0
    r = d_inner // gn
    l8 = L // 8

    def rows(b, c):
        return b * nc + c

    def halo_rows(b, c):
        return jnp.maximum((b * nc + c) * l8 - 1, 0)

    pad = LANES - heads
    dtb = jnp.pad(dt_bias.astype(F32), (0, pad)).reshape(1, LANES)
    alog = jnp.pad(a_log.astype(F32), (0, pad)).reshape(1, LANES)
    dskip = jnp.repeat(d_skip.astype(F32), M_HEADDIM).reshape(1, d_inner)
    cb2 = conv_b.reshape(1, -1)

    in_specs = [
        pl.BlockSpec((L, d_inner), lambda b, c: (rows(b, c), 0)),
        pl.BlockSpec((L, d_inner), lambda b, c: (rows(b, c), 1)),
        pl.BlockSpec((L, gn), lambda b, c: (rows(b, c), 2 * r)),
        pl.BlockSpec((L, gn), lambda b, c: (rows(b, c), 2 * r + 1)),
        pl.BlockSpec((8, d_inner), lambda b, c: (halo_rows(b, c), 1)),
        pl.BlockSpec((8, gn), lambda b, c: (halo_rows(b, c), 2 * r)),
        pl.BlockSpec((8, gn), lambda b, c: (halo_rows(b, c), 2 * r + 1)),
        pl.BlockSpec((L, LANES), lambda b, c: (rows(b, c), 0)),
        pl.BlockSpec((M_CONV, d_inner), lambda b, c: (0, 0)),
        pl.BlockSpec((M_CONV, gn), lambda b, c: (0, r)),
        pl.BlockSpec((M_CONV, gn), lambda b, c: (0, r + 1)),
        pl.BlockSpec((1, d_inner), lambda b, c: (0, 0)),
        pl.BlockSpec((1, gn), lambda b, c: (0, r)),
        pl.BlockSpec((1, gn), lambda b, c: (0, r + 1)),
        pl.BlockSpec((1, LANES), lambda b, c: (0, 0)),
        pl.BlockSpec((1, LANES), lambda b, c: (0, 0)),
        pl.BlockSpec((1, d_inner), lambda b, c: (0, 0)),
        pl.BlockSpec((1, d_inner), lambda b, c: (0, 0)),
        pl.BlockSpec((2 * LANES, d_inner), lambda b, c: (0, 0)),
        pl.BlockSpec((L, M_CONV * (CONV_PAD + L)), lambda b, c: (0, 0)),
    ]
    return pl.pallas_call(
        functools.partial(_ssd_kernel, hpg=hpg),
        grid=(batch, nc),
        in_specs=in_specs,
        out_specs=pl.BlockSpec((L, d_inner), lambda b, c: (rows(b, c), 0)),
        out_shape=jax.ShapeDtypeStruct((t, d_inner), BF16),
        scratch_shapes=[
            pltpu.VMEM((L, d_inner), F32),
            pltpu.VMEM((L, d_inner), BF16),
            pltpu.VMEM((L, d_inner), BF16),
            pltpu.VMEM((L, d_inner), F32),
            pltpu.VMEM((L, gn), F32),
            pltpu.VMEM((L, gn), BF16),
            pltpu.VMEM((L, d_inner), F32),
            pltpu.VMEM((M_D_STATE, d_inner), F32),
            pltpu.VMEM((LANES, L), F32),
            pltpu.VMEM((LANES, L), F32),
            pltpu.VMEM((M_GROUPS, L, LANES), F32),
            pltpu.VMEM((M_GROUPS, L // 2, LANES), F32),
        ],
        compiler_params=_cparams(("parallel", "arbitrary")),
        name="ssd_core",
    )(zx, zx, zx, zx, zx, zx, zx, dt, conv_w, conv_w, conv_w, cb2, cb2, cb2, dtb, alog, dskip,
      norm_w.reshape(1, d_inner), _head_expander(heads, d_inner), _conv_shift_matrix(L))


def kernel(x, mix_norm, ffn_norm, final_norm, hgrn_w_in, hgrn_lb_logits, hgrn_gnorm, hgrn_w_out,
           m_w_in, m_conv_w, m_conv_b, m_dt_bias, m_A_log, m_D, m_norm, m_w_out,
           f_w_up, f_conv_w, f_conv_b, f_w_down):
    batch, seq_len, d = x.shape
    depth = mix_norm.shape[0]
    t = batch * seq_len

    lb_p = jax.nn.softmax(hgrn_lb_logits.astype(F32), axis=0)
    lower_bounds = jnp.cumsum(lb_p, axis=0) - lb_p[0]

    d_inner = m_w_out.shape[1]
    m_heads = d_inner // M_HEADDIM
    zx_dim = m_w_in.shape[2] - m_heads

    hgrn_w_in_b, hgrn_w_out_b = hgrn_w_in.astype(BF16), hgrn_w_out.astype(BF16)
    m_w_in_b, m_w_out_b = cast_columns(m_w_in, zx_dim, BF16), m_w_out.astype(BF16)
    f_w_up_b, f_w_down_b = f_w_up.astype(BF16), f_w_down.astype(BF16)
    w_dt_b = jnp.pad(m_w_in[:, :, zx_dim:], ((0, 0), (0, 0), (0, LANES - m_heads))).astype(BF16)

    h = x.reshape(t, d)
    u = rmsnorm(h, mix_norm[0], BF16)
    out = None
    for i in range(depth):
        j = i // 2
        if i % 2 == 0:
            proj = matmul_headmajor(u, hgrn_w_in_b, j)
            y = hgrn_core(proj, lower_bounds[j], hgrn_gnorm[j], batch, seq_len)
            h, u = matmul_res_norm(y, hgrn_w_out_b, j, h, ffn_norm[i], BF16)
        else:
            zx = matmul(u, m_w_in_b, j, zx_dim)
            dt = matmul(u, w_dt_b, j, LANES, tn=LANES)
            y = ssd_core(zx, dt, m_conv_w[j], m_conv_b[j], m_dt_bias[j], m_A_log[j], m_D[j], m_norm[j],
                         batch, seq_len)
            h, u = matmul_res_norm(y, m_w_out_b, j, h, ffn_norm[i], BF16)
        a = ffn_up(u, f_w_up_b, i, f_conv_w[i], f_conv_b[i], seq_len)
        if i + 1 < depth:
            h, u = matmul_res_norm(a, f_w_down_b, i, h, mix_norm[i + 1], BF16)
        else:
            h, out = matmul_res_norm(a, f_w_down_b, i, h, final_norm, F32)
    return out.reshape(batch, seq_len, d)
```
